```python
import jax, jax.numpy as jnp
from jax import lax
import numpy as np

D_MODEL = 1024
BATCH = 2
SEQ = 8192
DEPTH = 4
DEC_BATCH = 32
DEC_SEQ = 64
PAST_LEN = 4096

CHUNK = 64
N_MIXERS = 2
E_A = D_MODEL
CONV_W = 31
CONV_STATE = CONV_W - 1
E_B = 2 * D_MODEL
H_B = 8
DH_B = E_B // H_B
SGU_CHUNK = 128
EPS = 1e-6
N_A = (DEPTH + 1) // 2
N_B = DEPTH // 2

kernel_name = "chunk_causal_conformer_gmlp_hybrid_step"


def rmsnorm(x, g):
    xf = x.astype(jnp.float32)
    y = xf * lax.rsqrt(jnp.mean(xf * xf, axis=-1, keepdims=True) + EPS)
    return (y * g.astype(jnp.float32)).astype(x.dtype)


def layernorm(x, g, b):
    xf = x.astype(jnp.float32)
    mu = jnp.mean(xf, axis=-1, keepdims=True)
    xc = xf - mu
    var = jnp.mean(xc * xc, axis=-1, keepdims=True)
    return (xc * lax.rsqrt(var + EPS) * g.astype(jnp.float32) + b.astype(jnp.float32)).astype(x.dtype)


def causal_dwconv(x_pad, w, b):
    out = lax.conv_general_dilated(
        x_pad, w[:, None, :].astype(x_pad.dtype), window_strides=(1,), padding='VALID',
        dimension_numbers=('NWC', 'WIO', 'NWC'), feature_group_count=x_pad.shape[-1])
    return out + b.astype(out.dtype)


def conv_mixer(h, hist, w_in, conv_w, conv_b, ln_g, ln_b, w_out):
    p = h @ w_in
    a, g_glu, z = jnp.split(p, 3, axis=-1)
    c = a * jax.nn.sigmoid(g_glu)
    c_pad = jnp.concatenate([hist.astype(c.dtype), c], axis=1)
    y = causal_dwconv(c_pad, conv_w, conv_b)
    y = jax.nn.silu(layernorm(y, ln_g, ln_b)) * jax.nn.silu(z)
    return y @ w_out, c_pad[:, -CONV_STATE:]


def block_causal_mask(n):
    cid = jnp.arange(n) // CHUNK
    return cid[None, :] <= cid[:, None]


def sgu_mixer(h, chunk_len, w_in, ln_g, ln_b, w_s, s_bias, w_out):
    B, T, _ = h.shape
    p = h @ w_in
    uv = jax.nn.gelu(p[..., :2 * E_B], approximate=False)
    z = p[..., 2 * E_B:]
    u, v = jnp.split(uv, 2, axis=-1)
    v = layernorm(v, ln_g, ln_b)
    nc = T // chunk_len
    u5 = u.reshape(B, nc, chunk_len, H_B, DH_B)
    v5 = v.reshape(B, nc, chunk_len, H_B, DH_B)
    ws = jnp.where(block_causal_mask(chunk_len)[None], w_s[:, :chunk_len, :chunk_len], 0).astype(v.dtype)
    mixed = jnp.einsum('gij,bcjgd->bcigd', ws, v5)
    mixed = mixed + jnp.transpose(s_bias[:, :chunk_len]).astype(v.dtype)[None, None, :, :, None]
    s = (u5 * mixed).reshape(B, T, E_B) * jax.nn.silu(z)
    return s @ w_out, v


def setup_inputs(seed: int = 0) -> dict:
    key = jax.random.key(seed)
    ks = jax.random.split(key, 20)
    f32 = jnp.float32
    n = jax.random.normal
    return {
        "x_prompt": n(ks[0], (BATCH, SEQ, D_MODEL), f32),
        "x_sample": n(ks[1], (DEC_BATCH, DEC_SEQ, D_MODEL), f32),
        "state_conv": n(ks[2], (N_A, DEC_BATCH, CONV_STATE, E_A), f32) * 0.5,
        "pre_norm_g": 1.0 + 0.02 * n(ks[3], (DEPTH, D_MODEL), f32),
        "post_norm_g": 1.0 + 0.02 * n(ks[4], (DEPTH, D_MODEL), f32),
        "a_w_in": n(ks[5], (N_A, D_MODEL, 3 * E_A), f32) * D_MODEL ** -0.5,
        "a_conv_w": n(ks[6], (N_A, CONV_W, E_A), f32) * CONV_W ** -0.5,
        "a_conv_b": 0.02 * n(ks[7], (N_A, E_A), f32),
        "a_ln_g": 1.0 + 0.02 * n(ks[8], (N_A, E_A), f32),
        "a_ln_b": 0.02 * n(ks[9], (N_A, E_A), f32),
        "a_w_out": n(ks[10], (N_A, E_A, D_MODEL), f32) * E_A ** -0.5,
        "b_w_in": n(ks[11], (N_B, D_MODEL, 3 * E_B), f32) * D_MODEL ** -0.5,
        "b_ln_g": 1.0 + 0.02 * n(ks[12], (N_B, E_B), f32),
        "b_ln_b": 0.02 * n(ks[13], (N_B, E_B), f32),
        "b_w_s": n(ks[14], (N_B, H_B, SGU_CHUNK, SGU_CHUNK), f32) * SGU_CHUNK ** -0.5,
        "b_s_bias": 1.0 + 0.02 * n(ks[15], (N_B, H_B, SGU_CHUNK), f32),
        "b_w_out": n(ks[16], (N_B, E_B, D_MODEL), f32) * E_B ** -0.5,
    }


def reference(x_prompt, x_sample, state_conv, pre_norm_g, post_norm_g,
              a_w_in, a_conv_w, a_conv_b, a_ln_g, a_ln_b, a_w_out,
              b_w_in, b_ln_g, b_ln_b, b_w_s, b_s_bias, b_w_out):
    xp, xs = x_prompt, x_sample
    conv_p, conv_s, v_s = [], [], []
    dec_len = xs.shape[1]
    for i in range(DEPTH):
        j = i // N_MIXERS
        hp = rmsnorm(xp, pre_norm_g[i])
        hs = rmsnorm(xs, pre_norm_g[i])
        if i % N_MIXERS == 0:
            zero_hist = jnp.zeros((xp.shape[0], CONV_STATE, E_A), xp.dtype)
            op, cp = conv_mixer(hp, zero_hist, a_w_in[j], a_conv_w[j], a_conv_b[j],
                                a_ln_g[j], a_ln_b[j], a_w_out[j])
            os_, cs = conv_mixer(hs, state_conv[j], a_w_in[j], a_conv_w[j], a_conv_b[j],
                                 a_ln_g[j], a_ln_b[j], a_w_out[j])
            conv_p.append(cp)
            conv_s.append(cs)
        else:
            op, _ = sgu_mixer(hp, SGU_CHUNK, b_w_in[j], b_ln_g[j], b_ln_b[j],
                              b_w_s[j], b_s_bias[j], b_w_out[j])
            os_, vs = sgu_mixer(hs, dec_len, b_w_in[j], b_ln_g[j], b_ln_b[j],
                                b_w_s[j], b_s_bias[j], b_w_out[j])
            v_s.append(vs)
        xp = xp + rmsnorm(op, post_norm_g[i])
        xs = xs + rmsnorm(os_, post_norm_g[i])
    new_conv_prompt = jnp.stack(conv_p)
    new_conv_sample = jnp.stack(conv_s)
    new_sgu_v_sample = jnp.stack(v_s)
    return (xp, xs, new_conv_prompt, new_conv_sample, new_sgu_v_sample)
```

```python
import functools

import jax
import jax.numpy as jnp
from jax import lax
from jax.experimental import pallas as pl
from jax.experimental.pallas import tpu as pltpu

EPS = 1e-6
CONV_W = 31
CONV_STATE = CONV_W - 1
HIST_ROWS = 32
HIST_OFF = HIST_ROWS - CONV_STATE
CHUNK = 64
SGU_CHUNK = 128
N_GROUPS = 8
LANES = 128
SUBLANES = 8
ROW_BLOCK = 64
CONV_ROWS = 16
VMEM_LIMIT_BYTES = 56 * 1024 * 1024


def _rms_scale(xf):
    return lax.rsqrt(jnp.mean(xf * xf, axis=-1, keepdims=True) + EPS)


def _layernorm(xf, g, b):
    mu = jnp.mean(xf, axis=-1, keepdims=True)
    xc = xf - mu
    var = jnp.mean(xc * xc, axis=-1, keepdims=True)
    return xc * lax.rsqrt(var + EPS) * g + b


def _gelu(x):
    return 0.5 * x * (1.0 + lax.erf(x * (2.0 ** -0.5)))


def _row_blocks(n_rows, rows_per_step, body):
    def step(i, carry):
        body(pl.multiple_of(i * rows_per_step, rows_per_step))
        return carry
    lax.fori_loop(0, n_rows // rows_per_step, step, 0)


def _prenorm_to_bf16(x_ref, g_ref, h_ref, n_rows):
    def body(r0):
        x = x_ref[pl.ds(r0, ROW_BLOCK), :]
        h_ref[pl.ds(r0, ROW_BLOCK), :] = (x * _rms_scale(x) * g_ref[...]).astype(h_ref.dtype)
    _row_blocks(n_rows, ROW_BLOCK, body)


def _postnorm_residual(x_ref, o_ref, g_ref, out_ref, n_rows):
    def body(r0):
        o = o_ref[pl.ds(r0, ROW_BLOCK), :]
        out_ref[pl.ds(r0, ROW_BLOCK), :] = x_ref[pl.ds(r0, ROW_BLOCK), :] + o * _rms_scale(o) * g_ref[...]
    _row_blocks(n_rows, ROW_BLOCK, body)


def _conv_layer_kernel(x_ref, hist_ref, g_pre_ref, w_in_ref, conv_w_ref, conv_b_ref,
                       ln_g_ref, ln_b_ref, w_out_ref, g_post_ref,
                       out_ref, new_hist_ref,
                       h_ref, p_ref, cbuf_ref, m_ref,
                       *, n_seg, seg_len, carry_history):
    tm = n_seg * seg_len
    e = conv_w_ref.shape[1]
    x2 = x_ref.at[0]
    out2 = out_ref.at[0]

    _prenorm_to_bf16(x2, g_pre_ref, h_ref, tm)

    p_ref[:, 0:2 * e] = jnp.dot(h_ref[...], w_in_ref[:, 0:2 * e], preferred_element_type=jnp.float32)

    if carry_history:
        t = pl.program_id(1)

        @pl.when(t == 0)
        def _():
            cbuf_ref[0, 0:HIST_OFF, :] = jnp.zeros((HIST_OFF, e), jnp.float32)
            cbuf_ref[0, HIST_OFF:HIST_ROWS, :] = hist_ref[0]

        @pl.when(t > 0)
        def _():
            cbuf_ref[0, 0:HIST_ROWS, :] = cbuf_ref[0, seg_len:seg_len + HIST_ROWS, :]
    else:
        for s in range(n_seg):
            cbuf_ref[s, 0:HIST_OFF, :] = jnp.zeros((HIST_OFF, e), jnp.float32)
            cbuf_ref[s, HIST_OFF:HIST_ROWS, :] = hist_ref[s]

    for s in range(n_seg):
        def glu(r0, s=s):
            rows = pl.ds(s * seg_len + r0, ROW_BLOCK)
            c = p_ref[rows, 0:e] * jax.nn.sigmoid(p_ref[rows, e:2 * e])
            cbuf_ref[s, pl.ds(HIST_ROWS + r0, ROW_BLOCK), :] = c
        _row_blocks(seg_len, ROW_BLOCK, glu)
        new_hist_ref[s] = cbuf_ref[s, HIST_ROWS + seg_len - CONV_STATE:HIST_ROWS + seg_len, :]

    win = ROW_BLOCK + SUBLANES
    for s in range(n_seg):
        for j in range(e // LANES):
            lanes = slice(j * LANES, (j + 1) * LANES)

            def conv(r0, s=s, lanes=lanes):
                cwin = cbuf_ref[s, pl.ds(r0, ROW_BLOCK + HIST_ROWS), lanes]
                acc = jnp.broadcast_to(conv_b_ref[:, lanes], (ROW_BLOCK, LANES))
                for sh in range(SUBLANES):
                    part = None
                    for a in range((CONV_STATE - sh) // SUBLANES + 1):
                        k = CONV_STATE - (SUBLANES * a + sh)
                        lo = HIST_ROWS - SUBLANES * (a + 1)
                        term = conv_w_ref[k:k + 1, lanes] * cwin[lo:lo + win, :]
                        part = term if part is None else part + term
                    acc = acc + part[SUBLANES - sh:SUBLANES - sh + ROW_BLOCK, :]
                p_ref[pl.ds(s * seg_len + r0, ROW_BLOCK), lanes] = acc
            _row_blocks(seg_len, ROW_BLOCK, conv)

    p_ref[:, e:2 * e] = jnp.dot(h_ref[...], w_in_ref[:, 2 * e:3 * e], preferred_element_type=jnp.float32)

    def gate(r0):
        rows = pl.ds(r0, ROW_BLOCK)
        y = _layernorm(p_ref[rows, 0:e], ln_g_ref[...], ln_b_ref[...])
        m_ref[rows, :] = (jax.nn.silu(y) * jax.nn.silu(p_ref[rows, e:2 * e])).astype(m_ref.dtype)
    _row_blocks(tm, ROW_BLOCK, gate)

    p_ref[:, 0:e] = jnp.dot(m_ref[...], w_out_ref[...], preferred_element_type=jnp.float32)
    _postnorm_residual(x2, p_ref.at[:, 0:e], g_post_ref, out2, tm)


def _conv_layer(x, hist, g_pre, w_in, conv_w, conv_b, ln_g, ln_b, w_out, g_post, *, tile_rows, seg_len, carry_history):
    n_groups, t_len, d = x.shape
    e = conv_w.shape[1]
    n_seg = tile_rows // seg_len
    n_tiles = t_len // tile_rows
    assert n_seg * seg_len == tile_rows and n_tiles * tile_rows == t_len
    assert seg_len % ROW_BLOCK == 0 and seg_len >= CONV_STATE
    if carry_history:
        assert n_seg == 1
        hist_map = lambda g, t: (g, 0, 0)
    else:
        hist_map = lambda g, t: (g * n_tiles + t, 0, 0)
    const2 = lambda g, t: (0, 0)
    single = pl.Buffered(1)
    kern = functools.partial(_conv_layer_kernel, n_seg=n_seg, seg_len=seg_len, carry_history=carry_history)
    n_hist = hist.shape[0]
    return pl.pallas_call(
        kern,
        grid=(n_groups, n_tiles),
        in_specs=[
            pl.BlockSpec((1, tile_rows, d), lambda g, t: (g, t, 0)),
            pl.BlockSpec((n_seg, CONV_STATE, e), hist_map),
            pl.BlockSpec((1, d), const2, pipeline_mode=single),
            pl.BlockSpec((d, 3 * e), const2, pipeline_mode=single),
            pl.BlockSpec((CONV_W, e), const2, pipeline_mode=single),
            pl.BlockSpec((1, e), const2, pipeline_mode=single),
            pl.BlockSpec((1, e), const2, pipeline_mode=single),
            pl.BlockSpec((1, e), const2, pipeline_mode=single),
            pl.BlockSpec((e, d), const2, pipeline_mode=single),
            pl.BlockSpec((1, d), const2, pipeline_mode=single),
        ],
        out_specs=[
            pl.BlockSpec((1, tile_rows, d), lambda g, t: (g, t, 0)),
            pl.BlockSpec((n_seg, CONV_STATE, e), hist_map),
        ],
        out_shape=[
            jax.ShapeDtypeStruct(x.shape, x.dtype),
            jax.ShapeDtypeStruct((n_hist, CONV_STATE, e), x.dtype),
        ],
        scratch_shapes=[
            pltpu.VMEM((tile_rows, d), jnp.bfloat16),
            pltpu.VMEM((tile_rows, 2 * e), jnp.float32),
            pltpu.VMEM((n_seg, HIST_ROWS + seg_len, e), jnp.float32),
            pltpu.VMEM((tile_rows, e), jnp.bfloat16),
        ],
        compiler_params=pltpu.CompilerParams(
            dimension_semantics=("arbitrary", "arbitrary"),
            vmem_limit_bytes=VMEM_LIMIT_BYTES),
        name="conv_layer_carry" if carry_history else "conv_layer_segments",
    )(x, hist, g_pre.reshape(1, d), w_in, conv_w, conv_b.reshape(1, e), ln_g.reshape(1, e),
      ln_b.reshape(1, e), w_out, g_post.reshape(1, d))


def _sgu_layer_kernel(x_ref, g_pre_ref, w_in_ref, ln_g_ref, ln_b_ref, ws_ref, bias_ref, w_out_ref, g_post_ref,
                      *rest, stream_rows, emit_v):
    if emit_v:
        out_ref, v_out_ref, h_ref, p_ref, s_ref = rest
    else:
        out_ref, h_ref, p_ref, s_ref = rest
        v_out_ref = None
    tm = x_ref.shape[0]
    eb = ln_g_ref.shape[1]
    dh = eb // N_GROUPS

    _prenorm_to_bf16(x_ref, g_pre_ref, h_ref, tm)
    p_ref[...] = jnp.dot(h_ref[...], w_in_ref[...], preferred_element_type=jnp.float32)

    row = lax.broadcasted_iota(jnp.int32, (SGU_CHUNK, SGU_CHUNK), 0)
    col = lax.broadcasted_iota(jnp.int32, (SGU_CHUNK, SGU_CHUNK), 1)
    visible = jnp.logical_and(row // stream_rows == col // stream_rows, col // CHUNK <= row // CHUNK)

    def mix(r0):
        rows = pl.ds(r0, SGU_CHUNK)
        v = _gelu(p_ref[rows, eb:2 * eb])
        v = _layernorm(v, ln_g_ref[...], ln_b_ref[...])
        if emit_v:
            v_out_ref[rows, :] = v
        vb = v.astype(jnp.bfloat16)
        for g in range(N_GROUPS):
            lanes = slice(g * dh, (g + 1) * dh)
            ws = jnp.where(visible, ws_ref[g], 0.0).astype(jnp.bfloat16)
            mixed = jnp.dot(ws, vb[:, lanes], preferred_element_type=jnp.float32) + bias_ref[:, lanes]
            u = _gelu(p_ref[rows, lanes])
            z = p_ref[rows, 2 * eb + g * dh:2 * eb + (g + 1) * dh]
            s_ref[rows, lanes] = (u * mixed * jax.nn.silu(z)).astype(s_ref.dtype)
    _row_blocks(tm, SGU_CHUNK, mix)

    p_ref[:, 0:x_ref.shape[1]] = jnp.dot(s_ref[...], w_out_ref[...], preferred_element_type=jnp.float32)
    _postnorm_residual(x_ref, p_ref.at[:, 0:x_ref.shape[1]], g_post_ref, out_ref, tm)


def _sgu_layer(x, g_pre, w_in, ln_g, ln_b, ws, bias_rows, w_out, g_post, *, tile_rows, stream_rows, emit_v):
    n_rows, d = x.shape
    eb = ln_g.shape[0]
    assert n_rows % tile_rows == 0 and tile_rows % SGU_CHUNK == 0
    const2 = lambda t: (0, 0)
    single = pl.Buffered(1)
    row_spec = pl.BlockSpec((tile_rows, d), lambda t: (t, 0))
    out_specs = [row_spec]
    out_shape = [jax.ShapeDtypeStruct(x.shape, x.dtype)]
    if emit_v:
        out_specs.append(pl.BlockSpec((tile_rows, eb), lambda t: (t, 0)))
        out_shape.append(jax.ShapeDtypeStruct((n_rows, eb), x.dtype))
    assert SGU_CHUNK % stream_rows == 0 and stream_rows % CHUNK == 0
    kern = functools.partial(_sgu_layer_kernel, stream_rows=stream_rows, emit_v=emit_v)
    return pl.pallas_call(
        kern,
        grid=(n_rows // tile_rows,),
        in_specs=[
            row_spec,
            pl.BlockSpec((1, d), const2, pipeline_mode=single),
            pl.BlockSpec((d, 3 * eb), const2, pipeline_mode=single),
            pl.BlockSpec((1, eb), const2, pipeline_mode=single),
            pl.BlockSpec((1, eb), const2, pipeline_mode=single),
            pl.BlockSpec((N_GROUPS, SGU_CHUNK, SGU_CHUNK), lambda t: (0, 0, 0), pipeline_mode=single),
            pl.BlockSpec((SGU_CHUNK, eb), const2, pipeline_mode=single),
            pl.BlockSpec((eb, d), const2, pipeline_mode=single),
            pl.BlockSpec((1, d), const2, pipeline_mode=single),
        ],
        out_specs=out_specs,
        out_shape=out_shape,
        scratch_shapes=[
            pltpu.VMEM((tile_rows, d), jnp.bfloat16),
            pltpu.VMEM((tile_rows, 3 * eb), jnp.float32),
            pltpu.VMEM((tile_rows, eb), jnp.bfloat16),
        ],
        compiler_params=pltpu.CompilerParams(
            dimension_semantics=("arbitrary",),
            vmem_limit_bytes=VMEM_LIMIT_BYTES),
        name="sgu_layer_emit_v" if emit_v else "sgu_layer",
    )(x, g_pre.reshape(1, d), w_in, ln_g.reshape(1, eb), ln_b.reshape(1, eb), ws, bias_rows, w_out,
      g_post.reshape(1, d))


def _bias_rows(s_bias, chunk_len, dh):
    b = jnp.tile(jnp.transpose(s_bias[:, :chunk_len]), (SGU_CHUNK // chunk_len, 1))
    return jnp.repeat(b, dh, axis=1)


def kernel(x_prompt, x_sample, state_conv, pre_norm_g, post_norm_g, a_w_in, a_conv_w, a_conv_b, a_ln_g, a_ln_b,
           a_w_out, b_w_in, b_ln_g, b_ln_b, b_w_s, b_s_bias, b_w_out):
    batch, seq, d = x_prompt.shape
    dec_batch, dec_seq, _ = x_sample.shape
    depth = pre_norm_g.shape[0]
    e_a = a_conv_w.shape[2]
    e_b = b_ln_g.shape[1]
    dh = e_b // N_GROUPS
    bf16 = jnp.bfloat16

    xp = x_prompt
    xs = x_sample.reshape(1, dec_batch * dec_seq, d)
    zero_hist = jnp.zeros((batch, CONV_STATE, e_a), x_prompt.dtype)
    conv_p, conv_s, v_s = [], [], []
    for i in range(depth):
        j = i // 2
        if i % 2 == 0:
            w_in = a_w_in[j].astype(bf16)
            w_out = a_w_out[j].astype(bf16)
            args = (pre_norm_g[i], w_in, a_conv_w[j], a_conv_b[j], a_ln_g[j], a_ln_b[j], w_out, post_norm_g[i])
            xp, cp = _conv_layer(xp, zero_hist, *args, tile_rows=512, seg_len=512, carry_history=True)
            xs, cs = _conv_layer(xs, state_conv[j], *args, tile_rows=512, seg_len=dec_seq, carry_history=False)
            conv_p.append(cp)
            conv_s.append(cs)
        else:
            w_in = b_w_in[j].astype(bf16)
            w_out = b_w_out[j].astype(bf16)
            reps = SGU_CHUNK // dec_seq
            ws_sample = jnp.tile(b_w_s[j][:, :dec_seq, :dec_seq], (1, reps, reps))
            xp2 = _sgu_layer(xp.reshape(batch * seq, d), pre_norm_g[i], w_in, b_ln_g[j], b_ln_b[j], b_w_s[j],
                             _bias_rows(b_s_bias[j], SGU_CHUNK, dh), w_out, post_norm_g[i],
                             tile_rows=256, stream_rows=SGU_CHUNK, emit_v=False)[0]
            xp = xp2.reshape(batch, seq, d)
            xs2, vs = _sgu_layer(xs.reshape(dec_batch * dec_seq, d), pre_norm_g[i], w_in, b_ln_g[j], b_ln_b[j],
                                 ws_sample, _bias_rows(b_s_bias[j], dec_seq, dh), w_out, post_norm_g[i],
                                 tile_rows=256, stream_rows=dec_seq, emit_v=True)
            xs = xs2.reshape(1, dec_batch * dec_seq, d)
            v_s.append(vs.reshape(dec_batch, dec_seq, e_b))
    return (xp, xs.reshape(dec_batch, dec_seq, d), jnp.stack(conv_p), jnp.stack(conv_s), jnp.stack(v_s))
```

```python
import functools

import jax
import jax.numpy as jnp
from jax import lax
from jax.experimental import pallas as pl
from jax.experimental.pallas import tpu as pltpu

EPS = 1e-6
CONV_W = 31
CONV_STATE = CONV_W - 1
HIST_ROWS = 32
HIST_OFF = HIST_ROWS - CONV_STATE
CHUNK = 64
SGU_CHUNK = 128
N_GROUPS = 8
LANES = 128
SUBLANES = 8
CONV_ROWS = 64
ROW_CHUNK = 128
TILE_ROWS = 512
VMEM_LIMIT_BYTES = 56 * 1024 * 1024


def _rms_scale(xf):
    return lax.rsqrt(jnp.mean(xf * xf, axis=-1, keepdims=True) + EPS)


def _layernorm(xf, g, b):
    mu = jnp.mean(xf, axis=-1, keepdims=True)
    xc = xf - mu
    var = jnp.mean(xc * xc, axis=-1, keepdims=True)
    return xc * lax.rsqrt(var + EPS) * g + b


def _sigmoid(x):
    return 0.5 * jnp.tanh(0.5 * x) + 0.5


def _silu(x):
    return x * _sigmoid(x)


def _gelu(x):
    return 0.5 * x * (1.0 + lax.erf(x * (2.0 ** -0.5)))


def _dot(a, b):
    return jnp.dot(a, b, preferred_element_type=jnp.float32)


def _conv_block(cbuf_ref, seg, r0, lanes, conv_w_ref, conv_b_ref):
    win = CONV_ROWS + SUBLANES
    cwin = cbuf_ref[seg, r0:r0 + CONV_ROWS + HIST_ROWS, lanes]
    acc = jnp.broadcast_to(conv_b_ref[:, lanes], (CONV_ROWS, LANES))
    for sh in range(SUBLANES):
        part = None
        for a in range((CONV_STATE - sh) // SUBLANES + 1):
            k = CONV_STATE - (SUBLANES * a + sh)
            lo = HIST_ROWS - SUBLANES * (a + 1)
            term = conv_w_ref[k:k + 1, lanes] * cwin[lo:lo + win, :]
            part = term if part is None else part + term
        acc = acc + part[SUBLANES - sh:SUBLANES - sh + CONV_ROWS, :]
    return acc


def _conv_layer_kernel(x_ref, hist_ref, g_pre_ref, w_in_ref, conv_w_ref, conv_b_ref,
                       ln_g_ref, ln_b_ref, w_out_ref, g_post_ref,
                       out_ref, new_hist_ref, cbuf_ref,
                       *, n_seg, seg_len, carry_history):
    tm = n_seg * seg_len
    e = conv_w_ref.shape[1]

    if carry_history:
        t = pl.program_id(1)

        @pl.when(t == 0)
        def _():
            cbuf_ref[0, 0:HIST_OFF, :] = jnp.zeros((HIST_OFF, e), jnp.float32)
            cbuf_ref[0, HIST_OFF:HIST_ROWS, :] = hist_ref[0]

        @pl.when(t > 0)
        def _():
            cbuf_ref[0, 0:HIST_ROWS, :] = cbuf_ref[0, seg_len:seg_len + HIST_ROWS, :]
    else:
        for s in range(n_seg):
            cbuf_ref[s, 0:HIST_OFF, :] = jnp.zeros((HIST_OFF, e), jnp.float32)
            cbuf_ref[s, HIST_OFF:HIST_ROWS, :] = hist_ref[s]

    for c in range(tm // ROW_CHUNK):
        rows = slice(c * ROW_CHUNK, (c + 1) * ROW_CHUNK)
        x = x_ref[0, rows, :]
        h = (x * _rms_scale(x) * g_pre_ref[...]).astype(jnp.bfloat16)
        glu = _dot(h, w_in_ref[:, 0:e]) * _sigmoid(_dot(h, w_in_ref[:, e:2 * e]))
        for b in range(ROW_CHUNK // CONV_ROWS):
            r = c * ROW_CHUNK + b * CONV_ROWS
            seg, r0 = divmod(r, seg_len)
            cbuf_ref[seg, HIST_ROWS + r0:HIST_ROWS + r0 + CONV_ROWS, :] = glu[b * CONV_ROWS:(b + 1) * CONV_ROWS, :]
        z = _dot(h, w_in_ref[:, 2 * e:3 * e])
        y_blocks = []
        for b in range(ROW_CHUNK // CONV_ROWS):
            seg, r0 = divmod(c * ROW_CHUNK + b * CONV_ROWS, seg_len)
            y_blocks.append(jnp.concatenate(
                [_conv_block(cbuf_ref, seg, r0, slice(j * LANES, (j + 1) * LANES), conv_w_ref, conv_b_ref)
                 for j in range(e // LANES)], axis=1))
        y = _layernorm(jnp.concatenate(y_blocks, axis=0), ln_g_ref[...], ln_b_ref[...])
        m = (_silu(y) * _silu(z)).astype(jnp.bfloat16)
        o = _dot(m, w_out_ref[...])
        out_ref[0, rows, :] = x + o * _rms_scale(o) * g_post_ref[...]

    for s in range(n_seg):
        new_hist_ref[s] = cbuf_ref[s, HIST_ROWS + seg_len - CONV_STATE:HIST_ROWS + seg_len, :]


def _conv_layer(x, hist, g_pre, w_in, conv_w, conv_b, ln_g, ln_b, w_out, g_post, *, seg_len, carry_history):
    n_groups, t_len, d = x.shape
    e = conv_w.shape[1]
    n_seg = TILE_ROWS // seg_len
    n_tiles = t_len // TILE_ROWS
    assert n_seg * seg_len == TILE_ROWS and n_tiles * TILE_ROWS == t_len
    assert seg_len % CONV_ROWS == 0 and seg_len >= CONV_STATE
    if carry_history:
        assert n_seg == 1
        hist_map = lambda g, t: (g, 0, 0)
    else:
        hist_map = lambda g, t: (g * n_tiles + t, 0, 0)
    const2 = lambda g, t: (0, 0)
    single = pl.Buffered(1)
    kern = functools.partial(_conv_layer_kernel, n_seg=n_seg, seg_len=seg_len, carry_history=carry_history)
    return pl.pallas_call(
        kern,
        grid=(n_groups, n_tiles),
        in_specs=[
            pl.BlockSpec((1, TILE_ROWS, d), lambda g, t: (g, t, 0)),
            pl.BlockSpec((n_seg, CONV_STATE, e), hist_map),
            pl.BlockSpec((1, d), const2, pipeline_mode=single),
            pl.BlockSpec((d, 3 * e), const2, pipeline_mode=single),
            pl.BlockSpec((CONV_W, e), const2, pipeline_mode=single),
            pl.BlockSpec((1, e), const2, pipeline_mode=single),
            pl.BlockSpec((1, e), const2, pipeline_mode=single),
            pl.BlockSpec((1, e), const2, pipeline_mode=single),
            pl.BlockSpec((e, d), const2, pipeline_mode=single),
            pl.BlockSpec((1, d), const2, pipeline_mode=single),
        ],
        out_specs=[
            pl.BlockSpec((1, TILE_ROWS, d), lambda g, t: (g, t, 0)),
            pl.BlockSpec((n_seg, CONV_STATE, e), hist_map),
        ],
        out_shape=[
            jax.ShapeDtypeStruct(x.shape, x.dtype),
            jax.ShapeDtypeStruct((hist.shape[0], CONV_STATE, e), x.dtype),
        ],
        scratch_shapes=[
            pltpu.VMEM((n_seg, HIST_ROWS + seg_len, e), jnp.float32),
        ],
        compiler_params=pltpu.CompilerParams(
            dimension_semantics=("arbitrary", "arbitrary"),
            vmem_limit_bytes=VMEM_LIMIT_BYTES),
        name="conv_layer_carry" if carry_history else "conv_layer_segments",
    )(x, hist, g_pre.reshape(1, d), w_in, conv_w, conv_b.reshape(1, e), ln_g.reshape(1, e),
      ln_b.reshape(1, e), w_out, g_post.reshape(1, d))


def _sgu_layer_kernel(x_ref, g_pre_ref, w_in_ref, ln_g_ref, ln_b_ref, ws_ref, bias_ref, w_out_ref, g_post_ref,
                      *rest, stream_rows, emit_v):
    if emit_v:
        out_ref, v_out_ref = rest
    else:
        out_ref, = rest
        v_out_ref = None
    tm = x_ref.shape[0]
    eb = ln_g_ref.shape[1]
    dh = eb // N_GROUPS

    row = lax.broadcasted_iota(jnp.int32, (SGU_CHUNK, SGU_CHUNK), 0)
    col = lax.broadcasted_iota(jnp.int32, (SGU_CHUNK, SGU_CHUNK), 1)
    visible = jnp.logical_and(row // stream_rows == col // stream_rows, col // CHUNK <= row // CHUNK)
    ws = [jnp.where(visible, ws_ref[g], 0.0).astype(jnp.bfloat16) for g in range(N_GROUPS)]

    for c in range(tm // SGU_CHUNK):
        rows = slice(c * SGU_CHUNK, (c + 1) * SGU_CHUNK)
        x = x_ref[rows, :]
        h = (x * _rms_scale(x) * g_pre_ref[...]).astype(jnp.bfloat16)
        v = _gelu(_dot(h, w_in_ref[:, eb:2 * eb]))
        v = _layernorm(v, ln_g_ref[...], ln_b_ref[...])
        if emit_v:
            v_out_ref[rows, :] = v
        vb = v.astype(jnp.bfloat16)
        s_parts = []
        for g in range(N_GROUPS):
            lanes = slice(g * dh, (g + 1) * dh)
            u = _gelu(_dot(h, w_in_ref[:, lanes]))
            z = _dot(h, w_in_ref[:, 2 * eb + g * dh:2 * eb + (g + 1) * dh])
            mixed = _dot(ws[g], vb[:, lanes]) + bias_ref[:, lanes]
            s_parts.append((u * mixed * _silu(z)).astype(jnp.bfloat16))
        o = _dot(jnp.concatenate(s_parts, axis=1), w_out_ref[...])
        out_ref[rows, :] = x + o * _rms_scale(o) * g_post_ref[...]


def _sgu_layer(x, g_pre, w_in, ln_g, ln_b, ws, bias_rows, w_out, g_post, *, stream_rows, emit_v):
    n_rows, d = x.shape
    eb = ln_g.shape[0]
    assert n_rows % TILE_ROWS == 0 and TILE_ROWS % SGU_CHUNK == 0
    assert SGU_CHUNK % stream_rows == 0 and stream_rows % CHUNK == 0
    const2 = lambda t: (0, 0)
    single = pl.Buffered(1)
    row_spec = pl.BlockSpec((TILE_ROWS, d), lambda t: (t, 0))
    out_specs = [row_spec]
    out_shape = [jax.ShapeDtypeStruct(x.shape, x.dtype)]
    if emit_v:
        out_specs.append(pl.BlockSpec((TILE_ROWS, eb), lambda t: (t, 0)))
        out_shape.append(jax.ShapeDtypeStruct((n_rows, eb), x.dtype))
    kern = functools.partial(_sgu_layer_kernel, stream_rows=stream_rows, emit_v=emit_v)
    return pl.pallas_call(
        kern,
        grid=(n_rows // TILE_ROWS,),
        in_specs=[
            row_spec,
            pl.BlockSpec((1, d), const2, pipeline_mode=single),
            pl.BlockSpec((d, 3 * eb), const2, pipeline_mode=single),
            pl.BlockSpec((1, eb), const2, pipeline_mode=single),
            pl.BlockSpec((1, eb), const2, pipeline_mode=single),
            pl.BlockSpec((N_GROUPS, SGU_CHUNK, SGU_CHUNK), lambda t: (0, 0, 0), pipeline_mode=single),
            pl.BlockSpec((SGU_CHUNK, eb), const2, pipeline_mode=single),
            pl.BlockSpec((eb, d), const2, pipeline_mode=single),
            pl.BlockSpec((1, d), const2, pipeline_mode=single),
        ],
        out_specs=out_specs,
        out_shape=out_shape,
        compiler_params=pltpu.CompilerParams(
            dimension_semantics=("arbitrary",),
            vmem_limit_bytes=VMEM_LIMIT_BYTES),
        name="sgu_layer_emit_v" if emit_v else "sgu_layer",
    )(x, g_pre.reshape(1, d), w_in, ln_g.reshape(1, eb), ln_b.reshape(1, eb), ws, bias_rows, w_out,
      g_post.reshape(1, d))


def _bias_rows(s_bias, chunk_len, dh):
    b = jnp.tile(jnp.transpose(s_bias[:, :chunk_len]), (SGU_CHUNK // chunk_len, 1))
    return jnp.repeat(b, dh, axis=1)


def kernel(x_prompt, x_sample, state_conv, pre_norm_g, post_norm_g, a_w_in, a_conv_w, a_conv_b, a_ln_g, a_ln_b,
           a_w_out, b_w_in, b_ln_g, b_ln_b, b_w_s, b_s_bias, b_w_out):
    batch, seq, d = x_prompt.shape
    dec_batch, dec_seq, _ = x_sample.shape
    depth = pre_norm_g.shape[0]
    e_a = a_conv_w.shape[2]
    e_b = b_ln_g.shape[1]
    dh = e_b // N_GROUPS
    bf16 = jnp.bfloat16

    xp = x_prompt
    xs = x_sample.reshape(1, dec_batch * dec_seq, d)
    zero_hist = jnp.zeros((batch, CONV_STATE, e_a), x_prompt.dtype)
    conv_p, conv_s, v_s = [], [], []
    for i in range(depth):
        j = i // 2
        if i % 2 == 0:
            w_in = a_w_in[j].astype(bf16)
            w_out = a_w_out[j].astype(bf16)
            args = (pre_norm_g[i], w_in, a_conv_w[j], a_conv_b[j], a_ln_g[j], a_ln_b[j], w_out, post_norm_g[i])
            xp, cp = _conv_layer(xp, zero_hist, *args, seg_len=TILE_ROWS, carry_history=True)
            xs, cs = _conv_layer(xs, state_conv[j], *args, seg_len=dec_seq, carry_history=False)
            conv_p.append(cp)
            conv_s.append(cs)
        else:
            w_in = b_w_in[j].astype(bf16)
            w_out = b_w_out[j].astype(bf16)
            reps = SGU_CHUNK // dec_seq
            ws_sample = jnp.tile(b_w_s[j][:, :dec_seq, :dec_seq], (1, reps, reps))
            xp2 = _sgu_layer(xp.reshape(batch * seq, d), pre_norm_g[i], w_in, b_ln_g[j], b_ln_b[j], b_w_s[j],
                             _bias_rows(b_s_bias[j], SGU_CHUNK, dh), w_out, post_norm_g[i],
                             stream_rows=SGU_CHUNK, emit_v=False)[0]
            xp = xp2.reshape(batch, seq, d)
            xs2, vs = _sgu_layer(xs.reshape(dec_batch * dec_seq, d), pre_norm_g[i], w_in, b_ln_g[j], b_ln_b[j],
                                 ws_sample, _bias_rows(b_s_bias[j], dec_seq, dh), w_out, post_norm_g[i],
                                 stream_rows=dec_seq, emit_v=True)
            xs = xs2.reshape(1, dec_batch * dec_seq, d)
            v_s.append(vs.reshape(dec_batch, dec_seq, e_b))
    return (xp, xs.reshape(dec_batch, dec_seq, d), jnp.stack(conv_p), jnp.stack(conv_s), jnp.stack(v_s))
```

```python
import functools

import jax
import jax.numpy as jnp
from jax import lax
from jax.experimental import pallas as pl
from jax.experimental.pallas import tpu as pltpu

EPS = 1e-6
CONV_W = 31
CONV_STATE = CONV_W - 1
HIST_ROWS = 32
HIST_OFF = HIST_ROWS - CONV_STATE
CHUNK = 64
SGU_CHUNK = 128
N_GROUPS = 8
LANES = 128
SUBLANES = 8
CONV_ROWS = 64
TILE_ROWS = 512
PACK_ROWS = 256
VMEM_LIMIT_BYTES = 56 * 1024 * 1024


def _rms_scale(xf):
    return lax.rsqrt(jnp.mean(xf * xf, axis=-1, keepdims=True) + EPS)


def _layernorm(xf, g, b):
    mu = jnp.mean(xf, axis=-1, keepdims=True)
    xc = xf - mu
    var = jnp.mean(xc * xc, axis=-1, keepdims=True)
    return xc * lax.rsqrt(var + EPS) * g + b


def _sigmoid(x):
    return 0.5 * jnp.tanh(0.5 * x) + 0.5


def _silu(x):
    return x * _sigmoid(x)


def _gelu(x):
    return 0.5 * x * (1.0 + lax.erf(x * (2.0 ** -0.5)))


def _dot(a, b):
    return jnp.dot(a, b, preferred_element_type=jnp.float32)


def _pack_bf16_kernel(w_ref, o_ref):
    o_ref[0] = pltpu.bitcast(w_ref[0].astype(jnp.bfloat16), jnp.uint32)


def _pack_bf16(w):
    n_layers, k, n = w.shape
    assert k % PACK_ROWS == 0
    return pl.pallas_call(
        _pack_bf16_kernel,
        grid=(n_layers, k // PACK_ROWS),
        in_specs=[pl.BlockSpec((1, PACK_ROWS, n), lambda l, i: (l, i, 0))],
        out_specs=pl.BlockSpec((1, PACK_ROWS // 2, n), lambda l, i: (l, i, 0)),
        out_shape=jax.ShapeDtypeStruct((n_layers, k // 2, n), jnp.uint32),
        compiler_params=pltpu.CompilerParams(dimension_semantics=("arbitrary", "arbitrary")),
        name="pack_bf16",
    )(w)


def _bf16_cols(w_ref, lo, hi):
    return pltpu.bitcast(w_ref[:, lo:hi], jnp.bfloat16)


def _conv_block(cbuf_ref, seg, r0, lanes, conv_w_ref, conv_b_ref):
    win = CONV_ROWS + SUBLANES
    cwin = cbuf_ref[seg, r0:r0 + CONV_ROWS + HIST_ROWS, lanes]
    acc = jnp.broadcast_to(conv_b_ref[:, lanes], (CONV_ROWS, LANES))
    for sh in range(SUBLANES):
        part = None
        for a in range((CONV_STATE - sh) // SUBLANES + 1):
            k = CONV_STATE - (SUBLANES * a + sh)
            lo = HIST_ROWS - SUBLANES * (a + 1)
            term = conv_w_ref[k:k + 1, lanes] * cwin[lo:lo + win, :]
            part = term if part is None else part + term
        acc = acc + part[SUBLANES - sh:SUBLANES - sh + CONV_ROWS, :]
    return acc


class _ConvRefs:
    def __init__(self, seg_len, cbuf_ref, g_pre_ref, w_in_ref, conv_w_ref, conv_b_ref, ln_g_ref, ln_b_ref,
                 w_out_ref, g_post_ref):
        self.seg_len, self.cbuf, self.g_pre, self.w_in, self.conv_w, self.conv_b = (
            seg_len, cbuf_ref, g_pre_ref, w_in_ref, conv_w_ref, conv_b_ref)
        self.ln_g, self.ln_b, self.w_out, self.g_post = ln_g_ref, ln_b_ref, w_out_ref, g_post_ref
        self.e = conv_w_ref.shape[1]

    def blocks(self, c):
        return [divmod(c * SGU_CHUNK + b * CONV_ROWS, self.seg_len) for b in range(SGU_CHUNK // CONV_ROWS)]


def _conv_front(x, c, p):
    e = p.e
    h = (x * _rms_scale(x) * p.g_pre[...]).astype(jnp.bfloat16)
    glu = _dot(h, _bf16_cols(p.w_in, 0, e)) * _sigmoid(_dot(h, _bf16_cols(p.w_in, e, 2 * e)))
    for b, (seg, r0) in enumerate(p.blocks(c)):
        p.cbuf[seg, HIST_ROWS + r0:HIST_ROWS + r0 + CONV_ROWS, :] = glu[b * CONV_ROWS:(b + 1) * CONV_ROWS, :]
    return _dot(h, _bf16_cols(p.w_in, 2 * e, 3 * e))


def _conv_taps(c, p, y_blocks):
    for seg, r0 in p.blocks(c):
        lane_blocks = []
        y_blocks.append(lane_blocks)
        for j in range(p.e // LANES):
            lane_blocks.append(_conv_block(p.cbuf, seg, r0, slice(j * LANES, (j + 1) * LANES), p.conv_w, p.conv_b))
            if j % 2 == 1:
                yield


def _conv_back(x, z, y_blocks, p):
    y = jnp.concatenate([jnp.concatenate(lane_blocks, axis=1) for lane_blocks in y_blocks], axis=0)
    y = _layernorm(y, p.ln_g[...], p.ln_b[...])
    m = (_silu(y) * _silu(z)).astype(jnp.bfloat16)
    o = _dot(m, _bf16_cols(p.w_out, 0, p.w_out.shape[1]))
    return x + o * _rms_scale(o) * p.g_post[...]


class _SguRefs:
    def __init__(self, ws, g_pre_ref, w_in_ref, ln_g_ref, ln_b_ref, bias_ref, w_out_ref, g_post_ref):
        self.ws, self.g_pre, self.w_in, self.ln_g, self.ln_b = ws, g_pre_ref, w_in_ref, ln_g_ref, ln_b_ref
        self.bias, self.w_out, self.g_post = bias_ref, w_out_ref, g_post_ref
        self.eb = ln_g_ref.shape[1]
        self.dh = self.eb // N_GROUPS


def _sgu_front(x, p):
    h = (x * _rms_scale(x) * p.g_pre[...]).astype(jnp.bfloat16)
    v = _gelu(_dot(h, _bf16_cols(p.w_in, p.eb, 2 * p.eb)))
    return h, _layernorm(v, p.ln_g[...], p.ln_b[...])


def _sgu_groups(h, vb, p, s_parts):
    for g in range(N_GROUPS):
        lanes = slice(g * p.dh, (g + 1) * p.dh)
        u = _gelu(_dot(h, _bf16_cols(p.w_in, g * p.dh, (g + 1) * p.dh)))
        z = _dot(h, _bf16_cols(p.w_in, 2 * p.eb + g * p.dh, 2 * p.eb + (g + 1) * p.dh))
        mixed = _dot(p.ws[g], vb[:, lanes]) + p.bias[:, lanes]
        s_parts.append((u * mixed * _silu(z)).astype(jnp.bfloat16))
        yield


def _sgu_back(x, s_parts, p):
    o = _dot(jnp.concatenate(s_parts, axis=1), _bf16_cols(p.w_out, 0, p.w_out.shape[1]))
    return x + o * _rms_scale(o) * p.g_post[...]


def _run_interleaved(*generators):
    live = list(generators)
    while live:
        for gen in list(live):
            if next(gen, StopIteration) is StopIteration:
                live.remove(gen)


def _layer_pair_kernel(x_ref, hist_ref,
                       a_g_pre_ref, a_w_in_ref, conv_w_ref, conv_b_ref, a_ln_g_ref, a_ln_b_ref, a_w_out_ref,
                       a_g_post_ref,
                       b_g_pre_ref, b_w_in_ref, b_ln_g_ref, b_ln_b_ref, ws_ref, bias_ref, b_w_out_ref, b_g_post_ref,
                       *rest, n_seg, seg_len, carry_history, emit_v):
    if emit_v:
        out_ref, new_hist_ref, v_out_ref, cbuf_ref = rest
    else:
        out_ref, new_hist_ref, cbuf_ref = rest
        v_out_ref = None
    tm = n_seg * seg_len
    e = conv_w_ref.shape[1]

    if carry_history:
        t = pl.program_id(1)

        @pl.when(t == 0)
        def _():
            cbuf_ref[0, 0:HIST_OFF, :] = jnp.zeros((HIST_OFF, e), jnp.float32)
            cbuf_ref[0, HIST_OFF:HIST_ROWS, :] = hist_ref[0]

        @pl.when(t > 0)
        def _():
            cbuf_ref[0, 0:HIST_ROWS, :] = cbuf_ref[0, seg_len:seg_len + HIST_ROWS, :]
    else:
        for s in range(n_seg):
            cbuf_ref[s, 0:HIST_OFF, :] = jnp.zeros((HIST_OFF, e), jnp.float32)
            cbuf_ref[s, HIST_OFF:HIST_ROWS, :] = hist_ref[s]

    stream_rows = min(seg_len, SGU_CHUNK)
    row = lax.broadcasted_iota(jnp.int32, (SGU_CHUNK, SGU_CHUNK), 0)
    col = lax.broadcasted_iota(jnp.int32, (SGU_CHUNK, SGU_CHUNK), 1)
    visible = jnp.logical_and(row // stream_rows == col // stream_rows, col // CHUNK <= row // CHUNK)
    ws = [jnp.where(visible, ws_ref[g], 0.0).astype(jnp.bfloat16) for g in range(N_GROUPS)]

    conv = _ConvRefs(seg_len, cbuf_ref, a_g_pre_ref, a_w_in_ref, conv_w_ref, conv_b_ref, a_ln_g_ref, a_ln_b_ref,
                     a_w_out_ref, a_g_post_ref)
    sgu = _SguRefs(ws, b_g_pre_ref, b_w_in_ref, b_ln_g_ref, b_ln_b_ref, bias_ref, b_w_out_ref, b_g_post_ref)
    n_chunks = tm // SGU_CHUNK

    def rows(c):
        return slice(c * SGU_CHUNK, (c + 1) * SGU_CHUNK)

    x_in = x_ref[0, rows(0), :]
    z = _conv_front(x_in, 0, conv)
    y_blocks = []
    _run_interleaved(_conv_taps(0, conv, y_blocks))
    x_mid = _conv_back(x_in, z, y_blocks, conv)
    for c in range(n_chunks):
        h, v = _sgu_front(x_mid, sgu)
        if emit_v:
            v_out_ref[0, rows(c), :] = v
        s_parts = []
        stages = [_sgu_groups(h, v.astype(jnp.bfloat16), sgu, s_parts)]
        if c + 1 < n_chunks:
            x_in = x_ref[0, rows(c + 1), :]
            z = _conv_front(x_in, c + 1, conv)
            y_blocks = []
            stages.append(_conv_taps(c + 1, conv, y_blocks))
        _run_interleaved(*stages)
        out_ref[0, rows(c), :] = _sgu_back(x_mid, s_parts, sgu)
        if c + 1 < n_chunks:
            x_mid = _conv_back(x_in, z, y_blocks, conv)

    for s in range(n_seg):
        new_hist_ref[s] = cbuf_ref[s, HIST_ROWS + seg_len - CONV_STATE:HIST_ROWS + seg_len, :]


def _layer_pair(x, hist, a_g_pre, a_w_in, conv_w, conv_b, a_ln_g, a_ln_b, a_w_out, a_g_post,
                b_g_pre, b_w_in, b_ln_g, b_ln_b, ws, bias_rows, b_w_out, b_g_post, *, seg_len, carry_history, emit_v):
    n_groups, t_len, d = x.shape
    e = conv_w.shape[1]
    eb = b_ln_g.shape[0]
    n_seg = TILE_ROWS // seg_len
    n_tiles = t_len // TILE_ROWS
    assert n_seg * seg_len == TILE_ROWS and n_tiles * TILE_ROWS == t_len
    assert seg_len % CONV_ROWS == 0 and seg_len >= CONV_STATE
    assert seg_len % CHUNK == 0 and (seg_len % SGU_CHUNK == 0 or SGU_CHUNK % seg_len == 0)
    if carry_history:
        assert n_seg == 1
        hist_map = lambda g, t: (g, 0, 0)
    else:
        hist_map = lambda g, t: (g * n_tiles + t, 0, 0)
    row_map = lambda g, t: (g, t, 0)

    def resident(shape):
        return pl.BlockSpec(shape, lambda g, t: (0,) * len(shape), pipeline_mode=pl.Buffered(1))

    out_specs = [pl.BlockSpec((1, TILE_ROWS, d), row_map), pl.BlockSpec((n_seg, CONV_STATE, e), hist_map)]
    out_shape = [jax.ShapeDtypeStruct(x.shape, x.dtype), jax.ShapeDtypeStruct((hist.shape[0], CONV_STATE, e), x.dtype)]
    if emit_v:
        out_specs.append(pl.BlockSpec((1, TILE_ROWS, eb), row_map))
        out_shape.append(jax.ShapeDtypeStruct((n_groups, t_len, eb), x.dtype))
    kern = functools.partial(_layer_pair_kernel, n_seg=n_seg, seg_len=seg_len, carry_history=carry_history,
                             emit_v=emit_v)
    return pl.pallas_call(
        kern,
        grid=(n_groups, n_tiles),
        in_specs=[
            pl.BlockSpec((1, TILE_ROWS, d), row_map),
            pl.BlockSpec((n_seg, CONV_STATE, e), hist_map),
            resident((1, d)), resident((d // 2, 3 * e)), resident((CONV_W, e)), resident((1, e)),
            resident((1, e)), resident((1, e)), resident((e // 2, d)), resident((1, d)),
            resident((1, d)), resident((d // 2, 3 * eb)), resident((1, eb)), resident((1, eb)),
            resident((N_GROUPS, SGU_CHUNK, SGU_CHUNK)), resident((SGU_CHUNK, eb)), resident((eb // 2, d)),
            resident((1, d)),
        ],
        out_specs=out_specs,
        out_shape=out_shape,
        scratch_shapes=[
            pltpu.VMEM((n_seg, HIST_ROWS + seg_len, e), jnp.float32),
        ],
        compiler_params=pltpu.CompilerParams(
            dimension_semantics=("arbitrary", "arbitrary"),
            vmem_limit_bytes=VMEM_LIMIT_BYTES),
        name="layer_pair_carry" if carry_history else "layer_pair_segments",
    )(x, hist, a_g_pre.reshape(1, d), a_w_in, conv_w, conv_b.reshape(1, e), a_ln_g.reshape(1, e),
      a_ln_b.reshape(1, e), a_w_out, a_g_post.reshape(1, d),
      b_g_pre.reshape(1, d), b_w_in, b_ln_g.reshape(1, eb), b_ln_b.reshape(1, eb), ws, bias_rows, b_w_out,
      b_g_post.reshape(1, d))


def _bias_rows(s_bias, chunk_len, dh):
    b = jnp.tile(jnp.transpose(s_bias[:, :chunk_len]), (SGU_CHUNK // chunk_len, 1))
    return jnp.repeat(b, dh, axis=1)


def kernel(x_prompt, x_sample, state_conv, pre_norm_g, post_norm_g, a_w_in, a_conv_w, a_conv_b, a_ln_g, a_ln_b,
           a_w_out, b_w_in, b_ln_g, b_ln_b, b_w_s, b_s_bias, b_w_out):
    batch, seq, d = x_prompt.shape
    dec_batch, dec_seq, _ = x_sample.shape
    depth = pre_norm_g.shape[0]
    assert depth % 2 == 0
    e_a = a_conv_w.shape[2]
    e_b = b_ln_g.shape[1]
    dh = e_b // N_GROUPS
    a_w_in_packed, a_w_out_packed = _pack_bf16(a_w_in), _pack_bf16(a_w_out)
    b_w_in_packed, b_w_out_packed = _pack_bf16(b_w_in), _pack_bf16(b_w_out)

    xp = x_prompt
    xs = x_sample.reshape(1, dec_batch * dec_seq, d)
    zero_hist = jnp.zeros((batch, CONV_STATE, e_a), x_prompt.dtype)
    reps = SGU_CHUNK // dec_seq
    conv_p, conv_s, v_s = [], [], []
    for j in range(depth // 2):
        ia, ib = 2 * j, 2 * j + 1
        a_args = (pre_norm_g[ia], a_w_in_packed[j], a_conv_w[j], a_conv_b[j], a_ln_g[j], a_ln_b[j],
                  a_w_out_packed[j], post_norm_g[ia])
        xp, cp = _layer_pair(
            xp, zero_hist, *a_args,
            pre_norm_g[ib], b_w_in_packed[j], b_ln_g[j], b_ln_b[j], b_w_s[j], _bias_rows(b_s_bias[j], SGU_CHUNK, dh),
            b_w_out_packed[j], post_norm_g[ib], seg_len=TILE_ROWS, carry_history=True, emit_v=False)
        ws_sample = jnp.tile(b_w_s[j][:, :dec_seq, :dec_seq], (1, reps, reps))
        xs, cs, vs = _layer_pair(
            xs, state_conv[j], *a_args,
            pre_norm_g[ib], b_w_in_packed[j], b_ln_g[j], b_ln_b[j], ws_sample, _bias_rows(b_s_bias[j], dec_seq, dh),
            b_w_out_packed[j], post_norm_g[ib], seg_len=dec_seq, carry_history=False, emit_v=True)
        conv_p.append(cp)
        conv_s.append(cs)
        v_s.append(vs.reshape(dec_batch, dec_seq, e_b))
    return (xp, xs.reshape(dec_batch, dec_seq, d), jnp.stack(conv_p), jnp.stack(conv_s), jnp.stack(v_s))
```

```python
import functools

import jax
import jax.numpy as jnp
from jax import lax
from jax.experimental import pallas as pl
from jax.experimental.pallas import tpu as pltpu

EPS = 1e-6
CONV_W = 31
CONV_STATE = CONV_W - 1
HIST_ROWS = 32
HIST_OFF = HIST_ROWS - CONV_STATE
CHUNK = 64
SGU_CHUNK = 128
N_GROUPS = 8
LANES = 128
SUBLANES = 8
CONV_ROWS = 64
TILE_ROWS = 512
PACK_ROWS = 256
DOT_COLS = 256
VMEM_LIMIT_BYTES = 56 * 1024 * 1024


def _rms_scale(xf):
    return lax.rsqrt(jnp.mean(xf * xf, axis=-1, keepdims=True) + EPS)


def _layernorm(xf, g, b):
    mu = jnp.mean(xf, axis=-1, keepdims=True)
    xc = xf - mu
    var = jnp.mean(xc * xc, axis=-1, keepdims=True)
    return xc * lax.rsqrt(var + EPS) * g + b


def _sigmoid(x):
    return 0.5 * jnp.tanh(0.5 * x) + 0.5


def _silu(x):
    return x * _sigmoid(x)


def _gelu(x):
    return 0.5 * x * (1.0 + lax.erf(x * (2.0 ** -0.5)))


def _dot(a, b):
    return jnp.dot(a, b, preferred_element_type=jnp.float32)


def _pack_bf16_kernel(w_ref, o_ref):
    o_ref[0] = pltpu.bitcast(w_ref[0].astype(jnp.bfloat16), jnp.uint32)


def _pack_bf16(w):
    n_layers, k, n = w.shape
    assert k % PACK_ROWS == 0
    return pl.pallas_call(
        _pack_bf16_kernel,
        grid=(n_layers, k // PACK_ROWS),
        in_specs=[pl.BlockSpec((1, PACK_ROWS, n), lambda l, i: (l, i, 0))],
        out_specs=pl.BlockSpec((1, PACK_ROWS // 2, n), lambda l, i: (l, i, 0)),
        out_shape=jax.ShapeDtypeStruct((n_layers, k // 2, n), jnp.uint32),
        compiler_params=pltpu.CompilerParams(dimension_semantics=("arbitrary", "arbitrary")),
        name="pack_bf16",
    )(w)


def _bf16_cols(w_ref, lo, hi):
    return pltpu.bitcast(w_ref[:, lo:hi], jnp.bfloat16)


def _conv_block(cbuf_ref, seg, r0, lanes, conv_w_ref, conv_b_ref):
    win = CONV_ROWS + SUBLANES
    cwin = cbuf_ref[seg, r0:r0 + CONV_ROWS + HIST_ROWS, lanes]
    acc = jnp.broadcast_to(conv_b_ref[:, lanes], (CONV_ROWS, LANES))
    for sh in range(SUBLANES):
        part = None
        for a in range((CONV_STATE - sh) // SUBLANES + 1):
            k = CONV_STATE - (SUBLANES * a + sh)
            lo = HIST_ROWS - SUBLANES * (a + 1)
            term = conv_w_ref[k:k + 1, lanes] * cwin[lo:lo + win, :]
            part = term if part is None else part + term
        acc = acc + part[SUBLANES - sh:SUBLANES - sh + CONV_ROWS, :]
    return acc


class _ConvRefs:
    def __init__(self, seg_len, cbuf_ref, g_pre_ref, w_in_ref, conv_w_ref, conv_b_ref, ln_g_ref, ln_b_ref,
                 w_out_ref, g_post_ref):
        self.seg_len, self.cbuf, self.g_pre, self.w_in, self.conv_w, self.conv_b = (
            seg_len, cbuf_ref, g_pre_ref, w_in_ref, conv_w_ref, conv_b_ref)
        self.ln_g, self.ln_b, self.w_out, self.g_post = ln_g_ref, ln_b_ref, w_out_ref, g_post_ref
        self.e = conv_w_ref.shape[1]

    def blocks(self, c):
        return [divmod(c * SGU_CHUNK + b * CONV_ROWS, self.seg_len) for b in range(SGU_CHUNK // CONV_ROWS)]


def _conv_layer_chunk(x, c, p):
    e = p.e
    h = (x * _rms_scale(x) * p.g_pre[...]).astype(jnp.bfloat16)
    blocks = p.blocks(c)
    y_cols = []
    for lo in range(0, e, DOT_COLS):
        hi = lo + DOT_COLS
        glu = _dot(h, _bf16_cols(p.w_in, lo, hi)) * _sigmoid(_dot(h, _bf16_cols(p.w_in, e + lo, e + hi)))
        for b, (seg, r0) in enumerate(blocks):
            p.cbuf[seg, HIST_ROWS + r0:HIST_ROWS + r0 + CONV_ROWS, lo:hi] = glu[b * CONV_ROWS:(b + 1) * CONV_ROWS, :]
        y_cols.append(jnp.concatenate(
            [jnp.concatenate([_conv_block(p.cbuf, seg, r0, slice(j, j + LANES), p.conv_w, p.conv_b)
                              for j in range(lo, hi, LANES)], axis=1)
             for seg, r0 in blocks], axis=0))
    z = _dot(h, _bf16_cols(p.w_in, 2 * e, 3 * e))
    y = _layernorm(jnp.concatenate(y_cols, axis=1), p.ln_g[...], p.ln_b[...])
    m = (_silu(y) * _silu(z)).astype(jnp.bfloat16)
    o = _dot(m, _bf16_cols(p.w_out, 0, p.w_out.shape[1]))
    return x + o * _rms_scale(o) * p.g_post[...]


class _SguRefs:
    def __init__(self, ws, g_pre_ref, w_in_ref, ln_g_ref, ln_b_ref, bias_ref, w_out_ref, g_post_ref):
        self.ws, self.g_pre, self.w_in, self.ln_g, self.ln_b = ws, g_pre_ref, w_in_ref, ln_g_ref, ln_b_ref
        self.bias, self.w_out, self.g_post = bias_ref, w_out_ref, g_post_ref
        self.eb = ln_g_ref.shape[1]
        self.dh = self.eb // N_GROUPS


def _sgu_layer_chunk(x, p):
    h = (x * _rms_scale(x) * p.g_pre[...]).astype(jnp.bfloat16)
    v = _gelu(_dot(h, _bf16_cols(p.w_in, p.eb, 2 * p.eb)))
    v = _layernorm(v, p.ln_g[...], p.ln_b[...])
    vb = v.astype(jnp.bfloat16)
    s_parts = []
    for g in range(N_GROUPS):
        lanes = slice(g * p.dh, (g + 1) * p.dh)
        u = _gelu(_dot(h, _bf16_cols(p.w_in, g * p.dh, (g + 1) * p.dh)))
        z = _dot(h, _bf16_cols(p.w_in, 2 * p.eb + g * p.dh, 2 * p.eb + (g + 1) * p.dh))
        mixed = _dot(p.ws[g], vb[:, lanes]) + p.bias[:, lanes]
        s_parts.append((u * mixed * _silu(z)).astype(jnp.bfloat16))
    o = _dot(jnp.concatenate(s_parts, axis=1), _bf16_cols(p.w_out, 0, p.w_out.shape[1]))
    return x + o * _rms_scale(o) * p.g_post[...], v


def _layer_pair_kernel(x_ref, hist_ref,
                       a_g_pre_ref, a_w_in_ref, conv_w_ref, conv_b_ref, a_ln_g_ref, a_ln_b_ref, a_w_out_ref,
                       a_g_post_ref,
                       b_g_pre_ref, b_w_in_ref, b_ln_g_ref, b_ln_b_ref, ws_ref, bias_ref, b_w_out_ref, b_g_post_ref,
                       *rest, n_seg, seg_len, carry_history, emit_v):
    if emit_v:
        out_ref, new_hist_ref, v_out_ref, cbuf_ref = rest
    else:
        out_ref, new_hist_ref, cbuf_ref = rest
        v_out_ref = None
    tm = n_seg * seg_len
    e = conv_w_ref.shape[1]

    if carry_history:
        t = pl.program_id(1)

        @pl.when(t == 0)
        def _():
            cbuf_ref[0, 0:HIST_OFF, :] = jnp.zeros((HIST_OFF, e), jnp.float32)
            cbuf_ref[0, HIST_OFF:HIST_ROWS, :] = hist_ref[0]

        @pl.when(t > 0)
        def _():
            cbuf_ref[0, 0:HIST_ROWS, :] = cbuf_ref[0, seg_len:seg_len + HIST_ROWS, :]
    else:
        for s in range(n_seg):
            cbuf_ref[s, 0:HIST_OFF, :] = jnp.zeros((HIST_OFF, e), jnp.float32)
            cbuf_ref[s, HIST_OFF:HIST_ROWS, :] = hist_ref[s]

    stream_rows = min(seg_len, SGU_CHUNK)
    row = lax.broadcasted_iota(jnp.int32, (SGU_CHUNK, SGU_CHUNK), 0)
    col = lax.broadcasted_iota(jnp.int32, (SGU_CHUNK, SGU_CHUNK), 1)
    visible = jnp.logical_and(row // stream_rows == col // stream_rows, col // CHUNK <= row // CHUNK)
    ws = [jnp.where(visible, ws_ref[g], 0.0).astype(jnp.bfloat16) for g in range(N_GROUPS)]

    conv = _ConvRefs(seg_len, cbuf_ref, a_g_pre_ref, a_w_in_ref, conv_w_ref, conv_b_ref, a_ln_g_ref, a_ln_b_ref,
                     a_w_out_ref, a_g_post_ref)
    sgu = _SguRefs(ws, b_g_pre_ref, b_w_in_ref, b_ln_g_ref, b_ln_b_ref, bias_ref, b_w_out_ref, b_g_post_ref)
    for c in range(tm // SGU_CHUNK):
        rows = slice(c * SGU_CHUNK, (c + 1) * SGU_CHUNK)
        x, v = _sgu_layer_chunk(_conv_layer_chunk(x_ref[0, rows, :], c, conv), sgu)
        out_ref[0, rows, :] = x
        if emit_v:
            v_out_ref[0, rows, :] = v

    for s in range(n_seg):
        new_hist_ref[s] = cbuf_ref[s, HIST_ROWS + seg_len - CONV_STATE:HIST_ROWS + seg_len, :]


def _layer_pair(x, hist, a_g_pre, a_w_in, conv_w, conv_b, a_ln_g, a_ln_b, a_w_out, a_g_post,
                b_g_pre, b_w_in, b_ln_g, b_ln_b, ws, bias_rows, b_w_out, b_g_post, *, layer, seg_len, carry_history,
                emit_v):
    n_groups, t_len, d = x.shape
    e = conv_w.shape[1]
    eb = b_ln_g.shape[0]
    n_seg = TILE_ROWS // seg_len
    n_tiles = t_len // TILE_ROWS
    assert n_seg * seg_len == TILE_ROWS and n_tiles * TILE_ROWS == t_len
    assert seg_len % CONV_ROWS == 0 and seg_len >= CONV_STATE
    assert seg_len % CHUNK == 0 and (seg_len % SGU_CHUNK == 0 or SGU_CHUNK % seg_len == 0)
    if carry_history:
        assert n_seg == 1
        hist_map = lambda g, t: (g, 0, 0)
    else:
        hist_map = lambda g, t: (g * n_tiles + t, 0, 0)
    row_map = lambda g, t: (g, t, 0)

    def resident(shape):
        return pl.BlockSpec(shape, lambda g, t: (0,) * len(shape), pipeline_mode=pl.Buffered(1))

    def resident_layer(shape):
        return pl.BlockSpec((None,) + shape, lambda g, t: (layer,) + (0,) * len(shape), pipeline_mode=pl.Buffered(1))

    out_specs = [pl.BlockSpec((1, TILE_ROWS, d), row_map), pl.BlockSpec((n_seg, CONV_STATE, e), hist_map)]
    out_shape = [jax.ShapeDtypeStruct(x.shape, x.dtype), jax.ShapeDtypeStruct((hist.shape[0], CONV_STATE, e), x.dtype)]
    if emit_v:
        out_specs.append(pl.BlockSpec((1, TILE_ROWS, eb), row_map))
        out_shape.append(jax.ShapeDtypeStruct((n_groups, t_len, eb), x.dtype))
    kern = functools.partial(_layer_pair_kernel, n_seg=n_seg, seg_len=seg_len, carry_history=carry_history,
                             emit_v=emit_v)
    return pl.pallas_call(
        kern,
        grid=(n_groups, n_tiles),
        in_specs=[
            pl.BlockSpec((1, TILE_ROWS, d), row_map),
            pl.BlockSpec((n_seg, CONV_STATE, e), hist_map),
            resident((1, d)), resident_layer((d // 2, 3 * e)), resident((CONV_W, e)), resident((1, e)),
            resident((1, e)), resident((1, e)), resident_layer((e // 2, d)), resident((1, d)),
            resident((1, d)), resident_layer((d // 2, 3 * eb)), resident((1, eb)), resident((1, eb)),
            resident((N_GROUPS, SGU_CHUNK, SGU_CHUNK)), resident((SGU_CHUNK, eb)), resident_layer((eb // 2, d)),
            resident((1, d)),
        ],
        out_specs=out_specs,
        out_shape=out_shape,
        scratch_shapes=[
            pltpu.VMEM((n_seg, HIST_ROWS + seg_len, e), jnp.float32),
        ],
        compiler_params=pltpu.CompilerParams(
            dimension_semantics=("arbitrary", "arbitrary"),
            vmem_limit_bytes=VMEM_LIMIT_BYTES),
        name="layer_pair_carry" if carry_history else "layer_pair_segments",
    )(x, hist, a_g_pre.reshape(1, d), a_w_in, conv_w, conv_b.reshape(1, e), a_ln_g.reshape(1, e),
      a_ln_b.reshape(1, e), a_w_out, a_g_post.reshape(1, d),
      b_g_pre.reshape(1, d), b_w_in, b_ln_g.reshape(1, eb), b_ln_b.reshape(1, eb), ws, bias_rows, b_w_out,
      b_g_post.reshape(1, d))


def _bias_rows(s_bias, chunk_len, dh):
    b = jnp.tile(jnp.transpose(s_bias[:, :chunk_len]), (SGU_CHUNK // chunk_len, 1))
    return jnp.repeat(b, dh, axis=1)


def kernel(x_prompt, x_sample, state_conv, pre_norm_g, post_norm_g, a_w_in, a_conv_w, a_conv_b, a_ln_g, a_ln_b,
           a_w_out, b_w_in, b_ln_g, b_ln_b, b_w_s, b_s_bias, b_w_out):
    batch, seq, d = x_prompt.shape
    dec_batch, dec_seq, _ = x_sample.shape
    depth = pre_norm_g.shape[0]
    assert depth % 2 == 0
    e_a = a_conv_w.shape[2]
    e_b = b_ln_g.shape[1]
    dh = e_b // N_GROUPS
    a_w_in_packed, a_w_out_packed = _pack_bf16(a_w_in), _pack_bf16(a_w_out)
    b_w_in_packed, b_w_out_packed = _pack_bf16(b_w_in), _pack_bf16(b_w_out)

    xp = x_prompt
    xs = x_sample.reshape(1, dec_batch * dec_seq, d)
    zero_hist = jnp.zeros((batch, CONV_STATE, e_a), x_prompt.dtype)
    reps = SGU_CHUNK // dec_seq
    conv_p, conv_s, v_s = [], [], []
    for j in range(depth // 2):
        ia, ib = 2 * j, 2 * j + 1
        a_args = (pre_norm_g[ia], a_w_in_packed, a_conv_w[j], a_conv_b[j], a_ln_g[j], a_ln_b[j],
                  a_w_out_packed, post_norm_g[ia])
        xp, cp = _layer_pair(
            xp, zero_hist, *a_args,
            pre_norm_g[ib], b_w_in_packed, b_ln_g[j], b_ln_b[j], b_w_s[j], _bias_rows(b_s_bias[j], SGU_CHUNK, dh),
            b_w_out_packed, post_norm_g[ib], layer=j, seg_len=TILE_ROWS, carry_history=True, emit_v=False)
        ws_sample = jnp.tile(b_w_s[j][:, :dec_seq, :dec_seq], (1, reps, reps))
        xs, cs, vs = _layer_pair(
            xs, state_conv[j], *a_args,
            pre_norm_g[ib], b_w_in_packed, b_ln_g[j], b_ln_b[j], ws_sample, _bias_rows(b_s_bias[j], dec_seq, dh),
            b_w_out_packed, post_norm_g[ib], layer=j, seg_len=dec_seq, carry_history=False, emit_v=True)
        conv_p.append(cp)
        conv_s.append(cs)
        v_s.append(vs.reshape(dec_batch, dec_seq, e_b))
    return (xp, xs.reshape(dec_batch, dec_seq, d), jnp.stack(conv_p), jnp.stack(conv_s), jnp.stack(v_s))
```

```python
import functools

import jax
import jax.numpy as jnp
from jax import lax
from jax.experimental import pallas as pl
from jax.experimental.pallas import tpu as pltpu

EPS = 1e-6
CONV_W = 31
CONV_STATE = CONV_W - 1
HIST_ROWS = 32
HIST_OFF = HIST_ROWS - CONV_STATE
CHUNK = 64
SGU_CHUNK = 128
N_GROUPS = 8
LANES = 128
SUBLANES = 8
CONV_ROWS = 64
TILE_ROWS = 512
PACK_ROWS = 256
DOT_COLS = 256
VMEM_LIMIT_BYTES = 56 * 1024 * 1024


def _rms_scale(xf):
    return lax.rsqrt(jnp.mean(xf * xf, axis=-1, keepdims=True) + EPS)


def _layernorm(xf, g, b):
    mu = jnp.mean(xf, axis=-1, keepdims=True)
    xc = xf - mu
    var = jnp.mean(xc * xc, axis=-1, keepdims=True)
    return xc * lax.rsqrt(var + EPS) * g + b


def _sigmoid(x):
    return 0.5 * jnp.tanh(0.5 * x) + 0.5


def _silu(x):
    return x * _sigmoid(x)


def _gelu(x):
    return 0.5 * x * (1.0 + lax.erf(x * (2.0 ** -0.5)))


def _dot(a, b):
    return jnp.dot(a, b, preferred_element_type=jnp.float32)


def _pack_bf16_kernel(w_ref, o_ref):
    o_ref[0] = pltpu.bitcast(w_ref[0].astype(jnp.bfloat16), jnp.uint32)


def _pack_bf16(w):
    n_layers, k, n = w.shape
    assert k % PACK_ROWS == 0
    return pl.pallas_call(
        _pack_bf16_kernel,
        grid=(n_layers, k // PACK_ROWS),
        in_specs=[pl.BlockSpec((1, PACK_ROWS, n), lambda l, i: (l, i, 0))],
        out_specs=pl.BlockSpec((1, PACK_ROWS // 2, n), lambda l, i: (l, i, 0)),
        out_shape=jax.ShapeDtypeStruct((n_layers, k // 2, n), jnp.uint32),
        compiler_params=pltpu.CompilerParams(dimension_semantics=("arbitrary", "arbitrary")),
        name="pack_bf16",
    )(w)


def _bf16_cols(w_ref, lo, hi):
    return pltpu.bitcast(w_ref[:, lo:hi], jnp.bfloat16)


def _conv_block(cbuf_ref, seg, r0, lanes, conv_w_ref, conv_b_ref, tie=None):
    win = CONV_ROWS + SUBLANES
    cwin = cbuf_ref[seg, r0:r0 + CONV_ROWS + HIST_ROWS, lanes]
    acc = jnp.broadcast_to(conv_b_ref[:, lanes], (CONV_ROWS, LANES))
    if tie is not None:
        acc = jnp.where(pl.program_id(1) >= 0, acc, tie)
    for sh in range(SUBLANES):
        part = None
        for a in range((CONV_STATE - sh) // SUBLANES + 1):
            k = CONV_STATE - (SUBLANES * a + sh)
            lo = HIST_ROWS - SUBLANES * (a + 1)
            term = conv_w_ref[k:k + 1, lanes] * cwin[lo:lo + win, :]
            part = term if part is None else part + term
        acc = acc + part[SUBLANES - sh:SUBLANES - sh + CONV_ROWS, :]
    return acc


class _ConvRefs:
    def __init__(self, seg_len, cbuf_ref, g_pre_ref, w_in_ref, conv_w_ref, conv_b_ref, ln_g_ref, ln_b_ref,
                 w_out_ref, g_post_ref):
        self.seg_len, self.cbuf, self.g_pre, self.w_in, self.conv_w, self.conv_b = (
            seg_len, cbuf_ref, g_pre_ref, w_in_ref, conv_w_ref, conv_b_ref)
        self.ln_g, self.ln_b, self.w_out, self.g_post = ln_g_ref, ln_b_ref, w_out_ref, g_post_ref
        self.e = conv_w_ref.shape[1]

    def blocks(self, c):
        return [divmod(c * SGU_CHUNK + b * CONV_ROWS, self.seg_len) for b in range(SGU_CHUNK // CONV_ROWS)]


def _conv_front(x, c, p):
    e = p.e
    h = (x * _rms_scale(x) * p.g_pre[...]).astype(jnp.bfloat16)
    glu = _dot(h, _bf16_cols(p.w_in, 0, e)) * _sigmoid(_dot(h, _bf16_cols(p.w_in, e, 2 * e)))
    for b, (seg, r0) in enumerate(p.blocks(c)):
        p.cbuf[seg, HIST_ROWS + r0:HIST_ROWS + r0 + CONV_ROWS, :] = glu[b * CONV_ROWS:(b + 1) * CONV_ROWS, :]
    return _dot(h, _bf16_cols(p.w_in, 2 * e, 3 * e))


def _conv_tap_block(c, p, j, b, tie_to=None):
    seg, r0 = p.blocks(c)[b]
    tie = None if tie_to is None else tie_to[0:CONV_ROWS, 0:LANES]
    return _conv_block(p.cbuf, seg, r0, slice(j * LANES, (j + 1) * LANES), p.conv_w, p.conv_b, tie)


def _conv_back(x, z, y_blocks, p):
    y = jnp.concatenate([jnp.concatenate(row_blocks, axis=0) for row_blocks in y_blocks], axis=1)
    y = _layernorm(y, p.ln_g[...], p.ln_b[...])
    m = (_silu(y) * _silu(z)).astype(jnp.bfloat16)
    o = _dot(m, _bf16_cols(p.w_out, 0, p.w_out.shape[1]))
    return x + o * _rms_scale(o) * p.g_post[...]


class _SguRefs:
    def __init__(self, ws, g_pre_ref, w_in_ref, ln_g_ref, ln_b_ref, bias_ref, w_out_ref, g_post_ref):
        self.ws, self.g_pre, self.w_in, self.ln_g, self.ln_b = ws, g_pre_ref, w_in_ref, ln_g_ref, ln_b_ref
        self.bias, self.w_out, self.g_post = bias_ref, w_out_ref, g_post_ref
        self.eb = ln_g_ref.shape[1]
        self.dh = self.eb // N_GROUPS


def _sgu_front(x, p, conv=None, c_next=None):
    h = (x * _rms_scale(x) * p.g_pre[...]).astype(jnp.bfloat16)
    v_cols, y_blocks = [], []
    for j in range(N_GROUPS):
        v_proj = _dot(h, _bf16_cols(p.w_in, p.eb + j * p.dh, p.eb + (j + 1) * p.dh))
        v_cols.append(_gelu(v_proj))
        if conv is not None:
            y_blocks.append(_conv_tap_block(c_next, conv, j, 0, v_proj))
    return h, _layernorm(jnp.concatenate(v_cols, axis=1), p.ln_g[...], p.ln_b[...]), y_blocks


def _sgu_groups(h, vb, p, conv=None, c_next=None):
    s_parts, y_blocks = [], []
    for g in range(N_GROUPS):
        lanes = slice(g * p.dh, (g + 1) * p.dh)
        u_proj = _dot(h, _bf16_cols(p.w_in, g * p.dh, (g + 1) * p.dh))
        z_proj = _dot(h, _bf16_cols(p.w_in, 2 * p.eb + g * p.dh, 2 * p.eb + (g + 1) * p.dh))
        mixed = _dot(p.ws[g], vb[:, lanes]) + p.bias[:, lanes]
        s_parts.append((_gelu(u_proj) * mixed * _silu(z_proj)).astype(jnp.bfloat16))
        if conv is not None:
            y_blocks.append(_conv_tap_block(c_next, conv, g, 1, u_proj))
    return s_parts, y_blocks


def _sgu_back(x, s_parts, p):
    o = _dot(jnp.concatenate(s_parts, axis=1), _bf16_cols(p.w_out, 0, p.w_out.shape[1]))
    return x + o * _rms_scale(o) * p.g_post[...]


def _layer_pair_kernel(x_ref, hist_ref,
                       a_g_pre_ref, a_w_in_ref, conv_w_ref, conv_b_ref, a_ln_g_ref, a_ln_b_ref, a_w_out_ref,
                       a_g_post_ref,
                       b_g_pre_ref, b_w_in_ref, b_ln_g_ref, b_ln_b_ref, ws_ref, bias_ref, b_w_out_ref, b_g_post_ref,
                       *rest, n_seg, seg_len, carry_history, emit_v):
    if emit_v:
        out_ref, new_hist_ref, v_out_ref, cbuf_ref = rest
    else:
        out_ref, new_hist_ref, cbuf_ref = rest
        v_out_ref = None
    tm = n_seg * seg_len
    e = conv_w_ref.shape[1]

    if carry_history:
        t = pl.program_id(1)

        @pl.when(t == 0)
        def _():
            cbuf_ref[0, 0:HIST_OFF, :] = jnp.zeros((HIST_OFF, e), jnp.float32)
            cbuf_ref[0, HIST_OFF:HIST_ROWS, :] = hist_ref[0]

        @pl.when(t > 0)
        def _():
            cbuf_ref[0, 0:HIST_ROWS, :] = cbuf_ref[0, seg_len:seg_len + HIST_ROWS, :]
    else:
        for s in range(n_seg):
            cbuf_ref[s, 0:HIST_OFF, :] = jnp.zeros((HIST_OFF, e), jnp.float32)
            cbuf_ref[s, HIST_OFF:HIST_ROWS, :] = hist_ref[s]

    stream_rows = min(seg_len, SGU_CHUNK)
    row = lax.broadcasted_iota(jnp.int32, (SGU_CHUNK, SGU_CHUNK), 0)
    col = lax.broadcasted_iota(jnp.int32, (SGU_CHUNK, SGU_CHUNK), 1)
    visible = jnp.logical_and(row // stream_rows == col // stream_rows, col // CHUNK <= row // CHUNK)
    ws = [jnp.where(visible, ws_ref[g], 0.0).astype(jnp.bfloat16) for g in range(N_GROUPS)]

    conv = _ConvRefs(seg_len, cbuf_ref, a_g_pre_ref, a_w_in_ref, conv_w_ref, conv_b_ref, a_ln_g_ref, a_ln_b_ref,
                     a_w_out_ref, a_g_post_ref)
    sgu = _SguRefs(ws, b_g_pre_ref, b_w_in_ref, b_ln_g_ref, b_ln_b_ref, bias_ref, b_w_out_ref, b_g_post_ref)
    n_chunks = tm // SGU_CHUNK

    def rows(c):
        return slice(c * SGU_CHUNK, (c + 1) * SGU_CHUNK)

    assert conv.e // LANES == N_GROUPS and SGU_CHUNK // CONV_ROWS == 2
    x_in = x_ref[0, rows(0), :]
    z = _conv_front(x_in, 0, conv)
    x_mid = _conv_back(x_in, z, [[_conv_tap_block(0, conv, j, b) for b in range(2)] for j in range(N_GROUPS)], conv)
    for c in range(n_chunks):
        c_next, ride = (c + 1, conv) if c + 1 < n_chunks else (None, None)
        if c_next is not None:
            x_in = x_ref[0, rows(c_next), :]
            z = _conv_front(x_in, c_next, conv)
        h, v, y_first = _sgu_front(x_mid, sgu, ride, c_next)
        if emit_v:
            v_out_ref[0, rows(c), :] = v
        s_parts, y_second = _sgu_groups(h, v.astype(jnp.bfloat16), sgu, ride, c_next)
        out_ref[0, rows(c), :] = _sgu_back(x_mid, s_parts, sgu)
        if c_next is not None:
            x_mid = _conv_back(x_in, z, [list(pair) for pair in zip(y_first, y_second)], conv)

    for s in range(n_seg):
        new_hist_ref[s] = cbuf_ref[s, HIST_ROWS + seg_len - CONV_STATE:HIST_ROWS + seg_len, :]


def _layer_pair(x, hist, a_g_pre, a_w_in, conv_w, conv_b, a_ln_g, a_ln_b, a_w_out, a_g_post,
                b_g_pre, b_w_in, b_ln_g, b_ln_b, ws, bias_rows, b_w_out, b_g_post, *, layer, seg_len, carry_history,
                emit_v):
    n_groups, t_len, d = x.shape
    e = conv_w.shape[1]
    eb = b_ln_g.shape[0]
    n_seg = TILE_ROWS // seg_len
    n_tiles = t_len // TILE_ROWS
    assert n_seg * seg_len == TILE_ROWS and n_tiles * TILE_ROWS == t_len
    assert seg_len % CONV_ROWS == 0 and seg_len >= CONV_STATE
    assert seg_len % CHUNK == 0 and (seg_len % SGU_CHUNK == 0 or SGU_CHUNK % seg_len == 0)
    if carry_history:
        assert n_seg == 1
        hist_map = lambda g, t: (g, 0, 0)
    else:
        hist_map = lambda g, t: (g * n_tiles + t, 0, 0)
    row_map = lambda g, t: (g, t, 0)

    def resident(shape):
        return pl.BlockSpec(shape, lambda g, t: (0,) * len(shape), pipeline_mode=pl.Buffered(1))

    def resident_layer(shape):
        return pl.BlockSpec((None,) + shape, lambda g, t: (layer,) + (0,) * len(shape), pipeline_mode=pl.Buffered(1))

    out_specs = [pl.BlockSpec((1, TILE_ROWS, d), row_map), pl.BlockSpec((n_seg, CONV_STATE, e), hist_map)]
    out_shape = [jax.ShapeDtypeStruct(x.shape, x.dtype), jax.ShapeDtypeStruct((hist.shape[0], CONV_STATE, e), x.dtype)]
    if emit_v:
        out_specs.append(pl.BlockSpec((1, TILE_ROWS, eb), row_map))
        out_shape.append(jax.ShapeDtypeStruct((n_groups, t_len, eb), x.dtype))
    kern = functools.partial(_layer_pair_kernel, n_seg=n_seg, seg_len=seg_len, carry_history=carry_history,
                             emit_v=emit_v)
    return pl.pallas_call(
        kern,
        grid=(n_groups, n_tiles),
        in_specs=[
            pl.BlockSpec((1, TILE_ROWS, d), row_map),
            pl.BlockSpec((n_seg, CONV_STATE, e), hist_map),
            resident((1, d)), resident_layer((d // 2, 3 * e)), resident((CONV_W, e)), resident((1, e)),
            resident((1, e)), resident((1, e)), resident_layer((e // 2, d)), resident((1, d)),
            resident((1, d)), resident_layer((d // 2, 3 * eb)), resident((1, eb)), resident((1, eb)),
            resident((N_GROUPS, SGU_CHUNK, SGU_CHUNK)), resident((SGU_CHUNK, eb)), resident_layer((eb // 2, d)),
            resident((1, d)),
        ],
        out_specs=out_specs,
        out_shape=out_shape,
        scratch_shapes=[
            pltpu.VMEM((n_seg, HIST_ROWS + seg_len, e), jnp.float32),
        ],
        compiler_params=pltpu.CompilerParams(
            dimension_semantics=("arbitrary", "arbitrary"),
            vmem_limit_bytes=VMEM_LIMIT_BYTES),
        name="layer_pair_carry" if carry_history else "layer_pair_segments",
    )(x, hist, a_g_pre.reshape(1, d), a_w_in, conv_w, conv_b.reshape(1, e), a_ln_g.reshape(1, e),
      a_ln_b.reshape(1, e), a_w_out, a_g_post.reshape(1, d),
      b_g_pre.reshape(1, d), b_w_in, b_ln_g.reshape(1, eb), b_ln_b.reshape(1, eb), ws, bias_rows, b_w_out,
      b_g_post.reshape(1, d))


def _bias_rows(s_bias, chunk_len, dh):
    b = jnp.tile(jnp.transpose(s_bias[:, :chunk_len]), (SGU_CHUNK // chunk_len, 1))
    return jnp.repeat(b, dh, axis=1)


def kernel(x_prompt, x_sample, state_conv, pre_norm_g, post_norm_g, a_w_in, a_conv_w, a_conv_b, a_ln_g, a_ln_b,
           a_w_out, b_w_in, b_ln_g, b_ln_b, b_w_s, b_s_bias, b_w_out):
    batch, seq, d = x_prompt.shape
    dec_batch, dec_seq, _ = x_sample.shape
    depth = pre_norm_g.shape[0]
    assert depth % 2 == 0
    e_a = a_conv_w.shape[2]
    e_b = b_ln_g.shape[1]
    dh = e_b // N_GROUPS
    a_w_in_packed, a_w_out_packed = _pack_bf16(a_w_in), _pack_bf16(a_w_out)
    b_w_in_packed, b_w_out_packed = _pack_bf16(b_w_in), _pack_bf16(b_w_out)

    xp = x_prompt
    xs = x_sample.reshape(1, dec_batch * dec_seq, d)
    zero_hist = jnp.zeros((batch, CONV_STATE, e_a), x_prompt.dtype)
    reps = SGU_CHUNK // dec_seq
    conv_p, conv_s, v_s = [], [], []
    for j in range(depth // 2):
        ia, ib = 2 * j, 2 * j + 1
        a_args = (pre_norm_g[ia], a_w_in_packed, a_conv_w[j], a_conv_b[j], a_ln_g[j], a_ln_b[j],
                  a_w_out_packed, post_norm_g[ia])
        xp, cp = _layer_pair(
            xp, zero_hist, *a_args,
            pre_norm_g[ib], b_w_in_packed, b_ln_g[j], b_ln_b[j], b_w_s[j], _bias_rows(b_s_bias[j], SGU_CHUNK, dh),
            b_w_out_packed, post_norm_g[ib], layer=j, seg_len=TILE_ROWS, carry_history=True, emit_v=False)
        ws_sample = jnp.tile(b_w_s[j][:, :dec_seq, :dec_seq], (1, reps, reps))
        xs, cs, vs = _layer_pair(
            xs, state_conv[j], *a_args,
            pre_norm_g[ib], b_w_in_packed, b_ln_g[j], b_ln_b[j], ws_sample, _bias_rows(b_s_bias[j], dec_seq, dh),
            b_w_out_packed, post_norm_g[ib], layer=j, seg_len=dec_seq, carry_history=False, emit_v=True)
        conv_p.append(cp)
        conv_s.append(cs)
        v_s.append(vs.reshape(dec_batch, dec_seq, e_b))
    return (xp, xs.reshape(dec_batch, dec_seq, d), jnp.stack(conv_p), jnp.stack(conv_s), jnp.stack(v_s))
```

```python
import functools

import jax
import jax.numpy as jnp
from jax import lax
from jax.experimental import pallas as pl
from jax.experimental.pallas import tpu as pltpu

EPS = 1e-6
CONV_W = 31
CONV_STATE = CONV_W - 1
HIST_ROWS = 32
HIST_OFF = HIST_ROWS - CONV_STATE
CHUNK = 64
SGU_CHUNK = 128
N_GROUPS = 8
LANES = 128
SUBLANES = 8
CONV_ROWS = 64
TILE_ROWS = 512
PACK_BLOCK_BYTES = 4 * 1024 * 1024
VMEM_LIMIT_BYTES = 56 * 1024 * 1024


def _rms_scale(xf):
    return lax.rsqrt(jnp.mean(xf * xf, axis=-1, keepdims=True) + EPS)


def _layernorm(xf, g, b):
    mu = jnp.mean(xf, axis=-1, keepdims=True)
    xc = xf - mu
    var = jnp.mean(xc * xc, axis=-1, keepdims=True)
    return xc * lax.rsqrt(var + EPS) * g + b


def _sigmoid(x):
    return 0.5 * jnp.tanh(0.5 * x) + 0.5


def _silu(x):
    return x * _sigmoid(x)


def _gelu(x):
    return 0.5 * x * (1.0 + lax.erf(x * (2.0 ** -0.5)))


def _dot(a, b):
    return jnp.dot(a, b, preferred_element_type=jnp.float32)


def _pack_bf16_kernel(w_ref, o_ref):
    o_ref[0] = pltpu.bitcast(w_ref[0].astype(jnp.bfloat16), jnp.uint32)


def _pack_bf16(w):
    n_layers, k, n = w.shape
    rows = k
    while rows * n * w.dtype.itemsize > PACK_BLOCK_BYTES and rows % (4 * SUBLANES) == 0:
        rows //= 2
    return pl.pallas_call(
        _pack_bf16_kernel,
        grid=(n_layers, k // rows),
        in_specs=[pl.BlockSpec((1, rows, n), lambda l, i: (l, i, 0))],
        out_specs=pl.BlockSpec((1, rows // 2, n), lambda l, i: (l, i, 0)),
        out_shape=jax.ShapeDtypeStruct((n_layers, k // 2, n), jnp.uint32),
        compiler_params=pltpu.CompilerParams(dimension_semantics=("arbitrary", "arbitrary")),
        name="pack_bf16",
    )(w)


def _bf16_cols(w_ref, lo, hi):
    return pltpu.bitcast(w_ref[:, lo:hi], jnp.bfloat16)


def _conv_block(cbuf_ref, seg, r0, lanes, conv_w_ref, conv_b_ref, tie=None):
    win = CONV_ROWS + SUBLANES
    cwin = cbuf_ref[seg, r0:r0 + CONV_ROWS + HIST_ROWS, lanes]
    acc = jnp.broadcast_to(conv_b_ref[:, lanes], (CONV_ROWS, LANES))
    if tie is not None:
        acc = jnp.where(pl.program_id(1) >= 0, acc, tie)
    for sh in range(SUBLANES):
        part = None
        for a in range((CONV_STATE - sh) // SUBLANES + 1):
            k = CONV_STATE - (SUBLANES * a + sh)
            lo = HIST_ROWS - SUBLANES * (a + 1)
            term = conv_w_ref[k:k + 1, lanes] * cwin[lo:lo + win, :]
            part = term if part is None else part + term
        acc = acc + part[SUBLANES - sh:SUBLANES - sh + CONV_ROWS, :]
    return acc


class _ConvRefs:
    def __init__(self, seg_len, cbuf_ref, g_pre_ref, w_in_ref, conv_w_ref, conv_b_ref, ln_g_ref, ln_b_ref,
                 w_out_ref, g_post_ref):
        self.seg_len, self.cbuf, self.g_pre, self.w_in, self.conv_w, self.conv_b = (
            seg_len, cbuf_ref, g_pre_ref, w_in_ref, conv_w_ref, conv_b_ref)
        self.ln_g, self.ln_b, self.w_out, self.g_post = ln_g_ref, ln_b_ref, w_out_ref, g_post_ref
        self.e = conv_w_ref.shape[1]

    def blocks(self, c):
        return [divmod(c * SGU_CHUNK + b * CONV_ROWS, self.seg_len) for b in range(SGU_CHUNK // CONV_ROWS)]


def _conv_front(x, c, p):
    e = p.e
    h = (x * _rms_scale(x) * p.g_pre[...]).astype(jnp.bfloat16)
    glu = _dot(h, _bf16_cols(p.w_in, 0, e)) * _sigmoid(_dot(h, _bf16_cols(p.w_in, e, 2 * e)))
    for b, (seg, r0) in enumerate(p.blocks(c)):
        p.cbuf[seg, HIST_ROWS + r0:HIST_ROWS + r0 + CONV_ROWS, :] = glu[b * CONV_ROWS:(b + 1) * CONV_ROWS, :]
    return _dot(h, _bf16_cols(p.w_in, 2 * e, 3 * e))


def _conv_tap_block(c, p, j, b, tie_to=None):
    seg, r0 = p.blocks(c)[b]
    tie = None if tie_to is None else tie_to[0:CONV_ROWS, 0:LANES]
    return _conv_block(p.cbuf, seg, r0, slice(j * LANES, (j + 1) * LANES), p.conv_w, p.conv_b, tie)


def _conv_back(x, z, y_blocks, p):
    y = jnp.concatenate([jnp.concatenate(row_blocks, axis=0) for row_blocks in y_blocks], axis=1)
    y = _layernorm(y, p.ln_g[...], p.ln_b[...])
    m = (_silu(y) * _silu(z)).astype(jnp.bfloat16)
    o = _dot(m, _bf16_cols(p.w_out, 0, p.w_out.shape[1]))
    return x + o * _rms_scale(o) * p.g_post[...]


class _SguRefs:
    def __init__(self, ws, g_pre_ref, w_in_ref, ln_g_ref, ln_b_ref, bias_ref, w_out_ref, g_post_ref):
        self.ws, self.g_pre, self.w_in, self.ln_g, self.ln_b = ws, g_pre_ref, w_in_ref, ln_g_ref, ln_b_ref
        self.bias, self.w_out, self.g_post = bias_ref, w_out_ref, g_post_ref
        self.eb = ln_g_ref.shape[1]
        self.dh = self.eb // N_GROUPS


def _sgu_front(x, p, conv=None, c_next=None):
    h = (x * _rms_scale(x) * p.g_pre[...]).astype(jnp.bfloat16)
    v_cols, y_blocks = [], []
    for j in range(N_GROUPS):
        v_proj = _dot(h, _bf16_cols(p.w_in, p.eb + j * p.dh, p.eb + (j + 1) * p.dh))
        v_cols.append(_gelu(v_proj))
        if conv is not None:
            y_blocks.append(_conv_tap_block(c_next, conv, j, 0, v_proj))
    return h, _layernorm(jnp.concatenate(v_cols, axis=1), p.ln_g[...], p.ln_b[...]), y_blocks


def _sgu_groups(h, vb, p, conv=None, c_next=None):
    s_parts, y_blocks = [], []
    for g in range(N_GROUPS):
        lanes = slice(g * p.dh, (g + 1) * p.dh)
        u_proj = _dot(h, _bf16_cols(p.w_in, g * p.dh, (g + 1) * p.dh))
        z_proj = _dot(h, _bf16_cols(p.w_in, 2 * p.eb + g * p.dh, 2 * p.eb + (g + 1) * p.dh))
        mixed = _dot(p.ws[g], vb[:, lanes]) + p.bias[:, lanes]
        s_parts.append((_gelu(u_proj) * mixed * _silu(z_proj)).astype(jnp.bfloat16))
        if conv is not None:
            y_blocks.append(_conv_tap_block(c_next, conv, g, 1, u_proj))
    return s_parts, y_blocks


def _sgu_back(x, s_parts, p):
    o = _dot(jnp.concatenate(s_parts, axis=1), _bf16_cols(p.w_out, 0, p.w_out.shape[1]))
    return x + o * _rms_scale(o) * p.g_post[...]


def _layer_pair_kernel(x_ref, hist_ref,
                       a_g_pre_ref, a_w_in_ref, conv_w_ref, conv_b_ref, a_ln_g_ref, a_ln_b_ref, a_w_out_ref,
                       a_g_post_ref,
                       b_g_pre_ref, b_w_in_ref, b_ln_g_ref, b_ln_b_ref, ws_ref, bias_ref, b_w_out_ref, b_g_post_ref,
                       *rest, n_seg, seg_len, carry_history, emit_v):
    if emit_v:
        out_ref, new_hist_ref, v_out_ref, cbuf_ref = rest
    else:
        out_ref, new_hist_ref, cbuf_ref = rest
        v_out_ref = None
    tm = n_seg * seg_len
    e = conv_w_ref.shape[1]

    if carry_history:
        t = pl.program_id(1)

        @pl.when(t == 0)
        def _():
            cbuf_ref[0, 0:HIST_OFF, :] = jnp.zeros((HIST_OFF, e), jnp.float32)
            cbuf_ref[0, HIST_OFF:HIST_ROWS, :] = hist_ref[0]

        @pl.when(t > 0)
        def _():
            cbuf_ref[0, 0:HIST_ROWS, :] = cbuf_ref[0, seg_len:seg_len + HIST_ROWS, :]
    else:
        for s in range(n_seg):
            cbuf_ref[s, 0:HIST_OFF, :] = jnp.zeros((HIST_OFF, e), jnp.float32)
            cbuf_ref[s, HIST_OFF:HIST_ROWS, :] = hist_ref[s]

    stream_rows = min(seg_len, SGU_CHUNK)
    row = lax.broadcasted_iota(jnp.int32, (SGU_CHUNK, SGU_CHUNK), 0)
    col = lax.broadcasted_iota(jnp.int32, (SGU_CHUNK, SGU_CHUNK), 1)
    visible = jnp.logical_and(row // stream_rows == col // stream_rows, col // CHUNK <= row // CHUNK)
    ws = [jnp.where(visible, ws_ref[g], 0.0).astype(jnp.bfloat16) for g in range(N_GROUPS)]

    conv = _ConvRefs(seg_len, cbuf_ref, a_g_pre_ref, a_w_in_ref, conv_w_ref, conv_b_ref, a_ln_g_ref, a_ln_b_ref,
                     a_w_out_ref, a_g_post_ref)
    sgu = _SguRefs(ws, b_g_pre_ref, b_w_in_ref, b_ln_g_ref, b_ln_b_ref, bias_ref, b_w_out_ref, b_g_post_ref)
    n_chunks = tm // SGU_CHUNK

    def rows(c):
        return slice(c * SGU_CHUNK, (c + 1) * SGU_CHUNK)

    assert conv.e // LANES == N_GROUPS and SGU_CHUNK // CONV_ROWS == 2
    x_in = x_ref[0, rows(0), :]
    z = _conv_front(x_in, 0, conv)
    x_mid = _conv_back(x_in, z, [[_conv_tap_block(0, conv, j, b) for b in range(2)] for j in range(N_GROUPS)], conv)
    for c in range(n_chunks):
        c_next, ride = (c + 1, conv) if c + 1 < n_chunks else (None, None)
        if c_next is not None:
            x_in = x_ref[0, rows(c_next), :]
            z = _conv_front(x_in, c_next, conv)
        h, v, y_first = _sgu_front(x_mid, sgu, ride, c_next)
        if emit_v:
            v_out_ref[0, rows(c), :] = v
        s_parts, y_second = _sgu_groups(h, v.astype(jnp.bfloat16), sgu, ride, c_next)
        out_ref[0, rows(c), :] = _sgu_back(x_mid, s_parts, sgu)
        if c_next is not None:
            x_mid = _conv_back(x_in, z, [list(pair) for pair in zip(y_first, y_second)], conv)

    for s in range(n_seg):
        new_hist_ref[s] = cbuf_ref[s, HIST_ROWS + seg_len - CONV_STATE:HIST_ROWS + seg_len, :]


def _layer_pair(x, hist, a_g_pre, a_w_in, conv_w, conv_b, a_ln_g, a_ln_b, a_w_out, a_g_post,
                b_g_pre, b_w_in, b_ln_g, b_ln_b, ws, bias_rows, b_w_out, b_g_post, *, layer, seg_len, carry_history,
                emit_v):
    n_groups, t_len, d = x.shape
    e = conv_w.shape[1]
    eb = b_ln_g.shape[0]
    n_seg = TILE_ROWS // seg_len
    n_tiles = t_len // TILE_ROWS
    assert n_seg * seg_len == TILE_ROWS and n_tiles * TILE_ROWS == t_len
    assert seg_len % CONV_ROWS == 0 and seg_len >= CONV_STATE
    assert seg_len % CHUNK == 0 and (seg_len % SGU_CHUNK == 0 or SGU_CHUNK % seg_len == 0)
    if carry_history:
        assert n_seg == 1
        hist_map = lambda g, t: (g, 0, 0)
    else:
        hist_map = lambda g, t: (g * n_tiles + t, 0, 0)
    row_map = lambda g, t: (g, t, 0)

    def resident(shape):
        return pl.BlockSpec(shape, lambda g, t: (0,) * len(shape), pipeline_mode=pl.Buffered(1))

    def resident_layer(shape):
        return pl.BlockSpec((None,) + shape, lambda g, t: (layer,) + (0,) * len(shape), pipeline_mode=pl.Buffered(1))

    out_specs = [pl.BlockSpec((1, TILE_ROWS, d), row_map), pl.BlockSpec((n_seg, CONV_STATE, e), hist_map)]
    out_shape = [jax.ShapeDtypeStruct(x.shape, x.dtype), jax.ShapeDtypeStruct((hist.shape[0], CONV_STATE, e), x.dtype)]
    if emit_v:
        out_specs.append(pl.BlockSpec((1, TILE_ROWS, eb), row_map))
        out_shape.append(jax.ShapeDtypeStruct((n_groups, t_len, eb), x.dtype))
    kern = functools.partial(_layer_pair_kernel, n_seg=n_seg, seg_len=seg_len, carry_history=carry_history,
                             emit_v=emit_v)
    return pl.pallas_call(
        kern,
        grid=(n_groups, n_tiles),
        in_specs=[
            pl.BlockSpec((1, TILE_ROWS, d), row_map),
            pl.BlockSpec((n_seg, CONV_STATE, e), hist_map),
            resident((1, d)), resident_layer((d // 2, 3 * e)), resident((CONV_W, e)), resident((1, e)),
            resident((1, e)), resident((1, e)), resident_layer((e // 2, d)), resident((1, d)),
            resident((1, d)), resident_layer((d // 2, 3 * eb)), resident((1, eb)), resident((1, eb)),
            resident((N_GROUPS, SGU_CHUNK, SGU_CHUNK)), resident((SGU_CHUNK, eb)), resident_layer((eb // 2, d)),
            resident((1, d)),
        ],
        out_specs=out_specs,
        out_shape=out_shape,
        scratch_shapes=[
            pltpu.VMEM((n_seg, HIST_ROWS + seg_len, e), jnp.float32),
        ],
        compiler_params=pltpu.CompilerParams(
            dimension_semantics=("arbitrary", "arbitrary"),
            vmem_limit_bytes=VMEM_LIMIT_BYTES),
        name="layer_pair_carry" if carry_history else "layer_pair_segments",
    )(x, hist, a_g_pre.reshape(1, d), a_w_in, conv_w, conv_b.reshape(1, e), a_ln_g.reshape(1, e),
      a_ln_b.reshape(1, e), a_w_out, a_g_post.reshape(1, d),
      b_g_pre.reshape(1, d), b_w_in, b_ln_g.reshape(1, eb), b_ln_b.reshape(1, eb), ws, bias_rows, b_w_out,
      b_g_post.reshape(1, d))


def _bias_rows(s_bias, chunk_len, dh):
    b = jnp.tile(jnp.transpose(s_bias[:, :chunk_len]), (SGU_CHUNK // chunk_len, 1))
    return jnp.repeat(b, dh, axis=1)


def kernel(x_prompt, x_sample, state_conv, pre_norm_g, post_norm_g, a_w_in, a_conv_w, a_conv_b, a_ln_g, a_ln_b,
           a_w_out, b_w_in, b_ln_g, b_ln_b, b_w_s, b_s_bias, b_w_out):
    batch, seq, d = x_prompt.shape
    dec_batch, dec_seq, _ = x_sample.shape
    depth = pre_norm_g.shape[0]
    assert depth % 2 == 0
    e_a = a_conv_w.shape[2]
    e_b = b_ln_g.shape[1]
    dh = e_b // N_GROUPS
    a_w_in_packed, a_w_out_packed = _pack_bf16(a_w_in), _pack_bf16(a_w_out)
    b_w_in_packed, b_w_out_packed = _pack_bf16(b_w_in), _pack_bf16(b_w_out)

    xp = x_prompt
    xs = x_sample.reshape(1, dec_batch * dec_seq, d)
    zero_hist = jnp.zeros((batch, CONV_STATE, e_a), x_prompt.dtype)
    reps = SGU_CHUNK // dec_seq
    conv_p, conv_s, v_s = [], [], []
    for j in range(depth // 2):
        ia, ib = 2 * j, 2 * j + 1
        a_args = (pre_norm_g[ia], a_w_in_packed, a_conv_w[j], a_conv_b[j], a_ln_g[j], a_ln_b[j],
                  a_w_out_packed, post_norm_g[ia])
        xp, cp = _layer_pair(
            xp, zero_hist, *a_args,
            pre_norm_g[ib], b_w_in_packed, b_ln_g[j], b_ln_b[j], b_w_s[j], _bias_rows(b_s_bias[j], SGU_CHUNK, dh),
            b_w_out_packed, post_norm_g[ib], layer=j, seg_len=TILE_ROWS, carry_history=True, emit_v=False)
        ws_sample = jnp.tile(b_w_s[j][:, :dec_seq, :dec_seq], (1, reps, reps))
        xs, cs, vs = _layer_pair(
            xs, state_conv[j], *a_args,
            pre_norm_g[ib], b_w_in_packed, b_ln_g[j], b_ln_b[j], ws_sample, _bias_rows(b_s_bias[j], dec_seq, dh),
            b_w_out_packed, post_norm_g[ib], layer=j, seg_len=dec_seq, carry_history=False, emit_v=True)
        conv_p.append(cp)
        conv_s.append(cs)
        v_s.append(vs.reshape(dec_batch, dec_seq, e_b))
    return (xp, xs.reshape(dec_batch, dec_seq, d), jnp.stack(conv_p), jnp.stack(conv_s), jnp.stack(v_s))
```

```python
import functools

import jax
import jax.numpy as jnp
from jax import lax
from jax.experimental import pallas as pl
from jax.experimental.pallas import tpu as pltpu

EPS = 1e-6
CONV_W = 31
CONV_STATE = CONV_W - 1
HIST_ROWS = 32
HIST_OFF = HIST_ROWS - CONV_STATE
CHUNK = 64
SGU_CHUNK = 128
N_GROUPS = 8
LANES = 128
SUBLANES = 8
CONV_ROWS = 64
TILE_ROWS = 512
PACK_BLOCK_BYTES = 4 * 1024 * 1024
VMEM_LIMIT_BYTES = 56 * 1024 * 1024


def _rms_scale(xf):
    return lax.rsqrt(jnp.mean(xf * xf, axis=-1, keepdims=True) + EPS)


def _layernorm(xf, g, b):
    mu = jnp.mean(xf, axis=-1, keepdims=True)
    xc = xf - mu
    var = jnp.mean(xc * xc, axis=-1, keepdims=True)
    return xc * lax.rsqrt(var + EPS) * g + b


def _sigmoid(x):
    return 0.5 * jnp.tanh(0.5 * x) + 0.5


def _silu(x):
    return x * _sigmoid(x)


def _gelu(x):
    return 0.5 * x * (1.0 + lax.erf(x * (2.0 ** -0.5)))


def _dot(a, b):
    return jnp.dot(a, b, preferred_element_type=jnp.float32)


def _pack_bf16_kernel(w_ref, o_ref):
    o_ref[0] = pltpu.bitcast(w_ref[0].astype(jnp.bfloat16), jnp.uint32)


def _pack_bf16(w):
    n_layers, k, n = w.shape
    rows = k
    while rows * n * w.dtype.itemsize > PACK_BLOCK_BYTES and rows % (4 * SUBLANES) == 0:
        rows //= 2
    return pl.pallas_call(
        _pack_bf16_kernel,
        grid=(n_layers, k // rows),
        in_specs=[pl.BlockSpec((1, rows, n), lambda l, i: (l, i, 0))],
        out_specs=pl.BlockSpec((1, rows // 2, n), lambda l, i: (l, i, 0)),
        out_shape=jax.ShapeDtypeStruct((n_layers, k // 2, n), jnp.uint32),
        compiler_params=pltpu.CompilerParams(dimension_semantics=("arbitrary", "arbitrary")),
        name="pack_bf16",
    )(w)


def _bf16_cols(w_ref, lo, hi):
    return pltpu.bitcast(w_ref[:, lo:hi], jnp.bfloat16)


def _conv_block(cbuf_ref, seg, r0, lanes, conv_w_ref, conv_b_ref, tie=None):
    win = CONV_ROWS + SUBLANES
    cwin = cbuf_ref[seg, r0:r0 + CONV_ROWS + HIST_ROWS, lanes]
    acc = jnp.broadcast_to(conv_b_ref[:, lanes], (CONV_ROWS, LANES))
    if tie is not None:
        acc = jnp.where(pl.program_id(1) >= 0, acc, tie)
    for sh in range(SUBLANES):
        part = None
        for a in range((CONV_STATE - sh) // SUBLANES + 1):
            k = CONV_STATE - (SUBLANES * a + sh)
            lo = HIST_ROWS - SUBLANES * (a + 1)
            term = conv_w_ref[k:k + 1, lanes] * cwin[lo:lo + win, :]
            part = term if part is None else part + term
        acc = acc + part[SUBLANES - sh:SUBLANES - sh + CONV_ROWS, :]
    return acc


class _ConvRefs:
    def __init__(self, seg_len, cbuf_ref, g_pre_ref, w_in_ref, conv_w_ref, conv_b_ref, ln_g_ref, ln_b_ref,
                 w_out_ref, g_post_ref):
        self.seg_len, self.cbuf, self.g_pre, self.w_in, self.conv_w, self.conv_b = (
            seg_len, cbuf_ref, g_pre_ref, w_in_ref, conv_w_ref, conv_b_ref)
        self.ln_g, self.ln_b, self.w_out, self.g_post = ln_g_ref, ln_b_ref, w_out_ref, g_post_ref
        self.e = conv_w_ref.shape[1]

    def blocks(self, c):
        return [divmod(c * SGU_CHUNK + b * CONV_ROWS, self.seg_len) for b in range(SGU_CHUNK // CONV_ROWS)]


def _conv_front(x, c, p):
    e = p.e
    h = (x * _rms_scale(x) * p.g_pre[...]).astype(jnp.bfloat16)
    glu = _dot(h, _bf16_cols(p.w_in, 0, e)) * _sigmoid(_dot(h, _bf16_cols(p.w_in, e, 2 * e)))
    for b, (seg, r0) in enumerate(p.blocks(c)):
        p.cbuf[seg, HIST_ROWS + r0:HIST_ROWS + r0 + CONV_ROWS, :] = glu[b * CONV_ROWS:(b + 1) * CONV_ROWS, :]
    return _dot(h, _bf16_cols(p.w_in, 2 * e, 3 * e))


def _conv_tap_block(c, p, j, b, tie_to=None):
    seg, r0 = p.blocks(c)[b]
    tie = None if tie_to is None else tie_to[0:CONV_ROWS, 0:LANES]
    return _conv_block(p.cbuf, seg, r0, slice(j * LANES, (j + 1) * LANES), p.conv_w, p.conv_b, tie)


def _conv_back(x, z, y_blocks, p):
    y = jnp.concatenate([jnp.concatenate(row_blocks, axis=0) for row_blocks in y_blocks], axis=1)
    y = _layernorm(y, p.ln_g[...], p.ln_b[...])
    m = (_silu(y) * _silu(z)).astype(jnp.bfloat16)
    o = _dot(m, _bf16_cols(p.w_out, 0, p.w_out.shape[1]))
    return x + o * _rms_scale(o) * p.g_post[...]


class _SguRefs:
    def __init__(self, ws, g_pre_ref, w_in_ref, ln_g_ref, ln_b_ref, bias_ref, w_out_ref, g_post_ref):
        self.ws, self.g_pre, self.w_in, self.ln_g, self.ln_b = ws, g_pre_ref, w_in_ref, ln_g_ref, ln_b_ref
        self.bias, self.w_out, self.g_post = bias_ref, w_out_ref, g_post_ref
        self.eb = ln_g_ref.shape[1]
        self.dh = self.eb // N_GROUPS


def _sgu_front(x, p, conv=None, c_next=None):
    h = (x * _rms_scale(x) * p.g_pre[...]).astype(jnp.bfloat16)
    v_cols, y_blocks = [], []
    for j in range(N_GROUPS):
        v_proj = _dot(h, _bf16_cols(p.w_in, p.eb + j * p.dh, p.eb + (j + 1) * p.dh))
        v_cols.append(_gelu(v_proj))
        if conv is not None:
            y_blocks.append(_conv_tap_block(c_next, conv, j, 0, v_proj))
    return h, _layernorm(jnp.concatenate(v_cols, axis=1), p.ln_g[...], p.ln_b[...]), y_blocks


def _sgu_groups(h, vb, p, conv=None, c_next=None):
    s_parts, y_blocks = [], []
    for g in range(N_GROUPS):
        lanes = slice(g * p.dh, (g + 1) * p.dh)
        u_proj = _dot(h, _bf16_cols(p.w_in, g * p.dh, (g + 1) * p.dh))
        z_proj = _dot(h, _bf16_cols(p.w_in, 2 * p.eb + g * p.dh, 2 * p.eb + (g + 1) * p.dh))
        mixed = _dot(p.ws[g], vb[:, lanes]) + p.bias[:, lanes]
        s_parts.append((_gelu(u_proj) * mixed * _silu(z_proj)).astype(jnp.bfloat16))
        if conv is not None:
            y_blocks.append(_conv_tap_block(c_next, conv, g, 1, u_proj))
    return s_parts, y_blocks


def _sgu_back(x, s_parts, p):
    o = _dot(jnp.concatenate(s_parts, axis=1), _bf16_cols(p.w_out, 0, p.w_out.shape[1]))
    return x + o * _rms_scale(o) * p.g_post[...]


def _layer_pair_kernel(x_ref, hist_ref,
                       a_g_pre_ref, a_w_in_ref, conv_w_ref, conv_b_ref, a_ln_g_ref, a_ln_b_ref, a_w_out_ref,
                       a_g_post_ref,
                       b_g_pre_ref, b_w_in_ref, b_ln_g_ref, b_ln_b_ref, ws_ref, bias_ref, b_w_out_ref, b_g_post_ref,
                       *rest, n_seg, seg_len, carry_history, emit_v):
    if emit_v:
        _, out_ref, new_hist_ref, v_out_ref, cbuf_ref = rest
    else:
        out_ref, new_hist_ref, cbuf_ref = rest
        v_out_ref = None
    tm = n_seg * seg_len
    e = conv_w_ref.shape[1]

    if carry_history:
        t = pl.program_id(1)

        @pl.when(t == 0)
        def _():
            cbuf_ref[0, 0:HIST_OFF, :] = jnp.zeros((HIST_OFF, e), jnp.float32)
            cbuf_ref[0, HIST_OFF:HIST_ROWS, :] = hist_ref[0]

        @pl.when(t > 0)
        def _():
            cbuf_ref[0, 0:HIST_ROWS, :] = cbuf_ref[0, seg_len:seg_len + HIST_ROWS, :]
    else:
        for s in range(n_seg):
            cbuf_ref[s, 0:HIST_OFF, :] = jnp.zeros((HIST_OFF, e), jnp.float32)
            cbuf_ref[s, HIST_OFF:HIST_ROWS, :] = hist_ref[s]

    stream_rows = min(seg_len, SGU_CHUNK)
    row = lax.broadcasted_iota(jnp.int32, (SGU_CHUNK, SGU_CHUNK), 0)
    col = lax.broadcasted_iota(jnp.int32, (SGU_CHUNK, SGU_CHUNK), 1)
    visible = jnp.logical_and(row // stream_rows == col // stream_rows, col // CHUNK <= row // CHUNK)
    ws = [jnp.where(visible, ws_ref[g], 0.0).astype(jnp.bfloat16) for g in range(N_GROUPS)]

    conv = _ConvRefs(seg_len, cbuf_ref, a_g_pre_ref, a_w_in_ref, conv_w_ref, conv_b_ref, a_ln_g_ref, a_ln_b_ref,
                     a_w_out_ref, a_g_post_ref)
    sgu = _SguRefs(ws, b_g_pre_ref, b_w_in_ref, b_ln_g_ref, b_ln_b_ref, bias_ref, b_w_out_ref, b_g_post_ref)
    n_chunks = tm // SGU_CHUNK

    def rows(c):
        return slice(c * SGU_CHUNK, (c + 1) * SGU_CHUNK)

    assert conv.e // LANES == N_GROUPS and SGU_CHUNK // CONV_ROWS == 2
    x_in = x_ref[0, rows(0), :]
    z = _conv_front(x_in, 0, conv)
    x_mid = _conv_back(x_in, z, [[_conv_tap_block(0, conv, j, b) for b in range(2)] for j in range(N_GROUPS)], conv)
    for c in range(n_chunks):
        c_next, ride = (c + 1, conv) if c + 1 < n_chunks else (None, None)
        if c_next is not None:
            x_in = x_ref[0, rows(c_next), :]
            z = _conv_front(x_in, c_next, conv)
        h, v, y_first = _sgu_front(x_mid, sgu, ride, c_next)
        if emit_v:
            v_out_ref[0, rows(c), :] = v
        s_parts, y_second = _sgu_groups(h, v.astype(jnp.bfloat16), sgu, ride, c_next)
        out_ref[0, rows(c), :] = _sgu_back(x_mid, s_parts, sgu)
        if c_next is not None:
            x_mid = _conv_back(x_in, z, [list(pair) for pair in zip(y_first, y_second)], conv)

    for s in range(n_seg):
        new_hist_ref[s] = cbuf_ref[s, HIST_ROWS + seg_len - CONV_STATE:HIST_ROWS + seg_len, :]


def _layer_pair(x, hist, a_g_pre, a_w_in, conv_w, conv_b, a_ln_g, a_ln_b, a_w_out, a_g_post,
                b_g_pre, b_w_in, b_ln_g, b_ln_b, ws, bias_rows, b_w_out, b_g_post, *, layer, seg_len, carry_history,
                emit_v, v_all=None):
    n_groups, t_len, d = x.shape
    e = conv_w.shape[1]
    eb = b_ln_g.shape[0]
    n_seg = TILE_ROWS // seg_len
    n_tiles = t_len // TILE_ROWS
    assert n_seg * seg_len == TILE_ROWS and n_tiles * TILE_ROWS == t_len
    assert seg_len % CONV_ROWS == 0 and seg_len >= CONV_STATE
    assert seg_len % CHUNK == 0 and (seg_len % SGU_CHUNK == 0 or SGU_CHUNK % seg_len == 0)
    if carry_history:
        assert n_seg == 1
        hist_map = lambda g, t: (g, 0, 0)
    else:
        hist_map = lambda g, t: (g * n_tiles + t, 0, 0)
    row_map = lambda g, t: (g, t, 0)

    def resident(shape):
        return pl.BlockSpec(shape, lambda g, t: (0,) * len(shape), pipeline_mode=pl.Buffered(1))

    def resident_layer(shape):
        return pl.BlockSpec((None,) + shape, lambda g, t: (layer,) + (0,) * len(shape), pipeline_mode=pl.Buffered(1))

    out_specs = [pl.BlockSpec((1, TILE_ROWS, d), row_map), pl.BlockSpec((n_seg, CONV_STATE, e), hist_map)]
    out_shape = [jax.ShapeDtypeStruct(x.shape, x.dtype), jax.ShapeDtypeStruct((hist.shape[0], CONV_STATE, e), x.dtype)]
    in_specs = [
        pl.BlockSpec((1, TILE_ROWS, d), row_map),
        pl.BlockSpec((n_seg, CONV_STATE, e), hist_map),
        resident((1, d)), resident_layer((d // 2, 3 * e)), resident((CONV_W, e)), resident((1, e)),
        resident((1, e)), resident((1, e)), resident_layer((e // 2, d)), resident((1, d)),
        resident((1, d)), resident_layer((d // 2, 3 * eb)), resident((1, eb)), resident((1, eb)),
        resident((N_GROUPS, SGU_CHUNK, SGU_CHUNK)), resident((SGU_CHUNK, eb)), resident_layer((eb // 2, d)),
        resident((1, d)),
    ]
    extra_args, aliases = (), {}
    if emit_v:
        assert v_all.shape[1:] == (n_groups, t_len, eb)
        aliases = {len(in_specs): len(out_specs)}
        in_specs.append(pl.BlockSpec(memory_space=pl.ANY))
        extra_args = (v_all,)
        out_specs.append(pl.BlockSpec((None, 1, TILE_ROWS, eb), lambda g, t: (layer, g, t, 0)))
        out_shape.append(jax.ShapeDtypeStruct(v_all.shape, v_all.dtype))
    kern = functools.partial(_layer_pair_kernel, n_seg=n_seg, seg_len=seg_len, carry_history=carry_history,
                             emit_v=emit_v)
    return pl.pallas_call(
        kern,
        grid=(n_groups, n_tiles),
        in_specs=in_specs,
        out_specs=out_specs,
        out_shape=out_shape,
        input_output_aliases=aliases,
        scratch_shapes=[
            pltpu.VMEM((n_seg, HIST_ROWS + seg_len, e), jnp.float32),
        ],
        compiler_params=pltpu.CompilerParams(
            dimension_semantics=("arbitrary", "arbitrary"),
            vmem_limit_bytes=VMEM_LIMIT_BYTES),
        name="layer_pair_carry" if carry_history else "layer_pair_segments",
    )(x, hist, a_g_pre.reshape(1, d), a_w_in, conv_w, conv_b.reshape(1, e), a_ln_g.reshape(1, e),
      a_ln_b.reshape(1, e), a_w_out, a_g_post.reshape(1, d),
      b_g_pre.reshape(1, d), b_w_in, b_ln_g.reshape(1, eb), b_ln_b.reshape(1, eb), ws, bias_rows, b_w_out,
      b_g_post.reshape(1, d), *extra_args)


def _bias_rows(s_bias, chunk_len, dh):
    b = jnp.tile(jnp.transpose(s_bias[:, :chunk_len]), (SGU_CHUNK // chunk_len, 1))
    return jnp.repeat(b, dh, axis=1)


def kernel(x_prompt, x_sample, state_conv, pre_norm_g, post_norm_g, a_w_in, a_conv_w, a_conv_b, a_ln_g, a_ln_b,
           a_w_out, b_w_in, b_ln_g, b_ln_b, b_w_s, b_s_bias, b_w_out):
    batch, seq, d = x_prompt.shape
    dec_batch, dec_seq, _ = x_sample.shape
    depth = pre_norm_g.shape[0]
    assert depth % 2 == 0
    e_a = a_conv_w.shape[2]
    e_b = b_ln_g.shape[1]
    dh = e_b // N_GROUPS
    a_w_in_packed, a_w_out_packed = _pack_bf16(a_w_in), _pack_bf16(a_w_out)
    b_w_in_packed, b_w_out_packed = _pack_bf16(b_w_in), _pack_bf16(b_w_out)

    xp = x_prompt
    xs = x_sample.reshape(1, dec_batch * dec_seq, d)
    zero_hist = jnp.zeros((batch, CONV_STATE, e_a), x_prompt.dtype)
    reps = SGU_CHUNK // dec_seq
    conv_p, conv_s = [], []
    v_all = jnp.zeros((depth // 2, 1, dec_batch * dec_seq, e_b), x_sample.dtype)
    for j in range(depth // 2):
        ia, ib = 2 * j, 2 * j + 1
        a_args = (pre_norm_g[ia], a_w_in_packed, a_conv_w[j], a_conv_b[j], a_ln_g[j], a_ln_b[j],
                  a_w_out_packed, post_norm_g[ia])
        xp, cp = _layer_pair(
            xp, zero_hist, *a_args,
            pre_norm_g[ib], b_w_in_packed, b_ln_g[j], b_ln_b[j], b_w_s[j], _bias_rows(b_s_bias[j], SGU_CHUNK, dh),
            b_w_out_packed, post_norm_g[ib], layer=j, seg_len=TILE_ROWS, carry_history=True, emit_v=False)
        ws_sample = jnp.tile(b_w_s[j][:, :dec_seq, :dec_seq], (1, reps, reps))
        xs, cs, v_all = _layer_pair(
            xs, state_conv[j], *a_args,
            pre_norm_g[ib], b_w_in_packed, b_ln_g[j], b_ln_b[j], ws_sample, _bias_rows(b_s_bias[j], dec_seq, dh),
            b_w_out_packed, post_norm_g[ib], layer=j, seg_len=dec_seq, carry_history=False, emit_v=True,
            v_all=v_all)
        conv_p.append(cp)
        conv_s.append(cs)
    return (xp, xs.reshape(dec_batch, dec_seq, d), jnp.stack(conv_p), jnp.stack(conv_s),
            v_all.reshape(depth // 2, dec_batch, dec_seq, e_b))
```
